```python
import math
import jax, jax.numpy as jnp
from jax import lax
import numpy as np

D_MODEL = 4096
BATCH = 8
SEQ = 2048
DEPTH = 4
DEC_BATCH = 4
DEC_SEQ = 2048
PAST_LEN = 128

D_FF = 256 * ((8 * D_MODEL // 3 + 255) // 256)
S5_WIDTH = D_MODEL // 4
S5_H = 16
S5_GROUPS = S5_WIDTH // S5_H
S5_P = 64
CONV_WIDTH = D_MODEL // 4
CONV_K = 31
CONV_PAD = (CONV_K - 1) // 2
ATTN_WIDTH = D_MODEL - S5_WIDTH - CONV_WIDTH
ATTN_DH = 128
ATTN_HEADS = ATTN_WIDTH // (2 * ATTN_DH)
Q_BLOCK = 128
MIX_WIDTH = S5_WIDTH + CONV_WIDTH + ATTN_WIDTH
IN_COLS = S5_WIDTH + 2 * CONV_WIDTH + 3 * ATTN_WIDTH
LN_EPS = 1e-5
ALPHA = (2 * DEPTH) ** 0.25
BETA = (8 * DEPTH) ** -0.25

kernel_name = 'hybrid_s5_conv_diffattn_encoder'


def layer_norm(x, g, b):
    xf = x.astype(jnp.float32)
    mu = jnp.mean(xf, axis=-1, keepdims=True)
    var = jnp.mean(jnp.square(xf - mu), axis=-1, keepdims=True)
    y = (xf - mu) * lax.rsqrt(var + LN_EPS) * g.astype(jnp.float32) + b.astype(jnp.float32)
    return y.astype(x.dtype)


def rms_norm(x, g):
    xf = x.astype(jnp.float32)
    y = xf * lax.rsqrt(jnp.mean(jnp.square(xf), axis=-1, keepdims=True) + LN_EPS) * g.astype(jnp.float32)
    return y.astype(x.dtype)


def swiglu(x, w_in, w_out):
    gate, up = jnp.split(x @ w_in, 2, axis=-1)
    return (jax.nn.silu(gate) * up) @ w_out


def alibi_slopes():
    return jnp.asarray([2.0 ** (-8.0 * (h + 1) / ATTN_HEADS) for h in range(ATTN_HEADS)], dtype=jnp.float32)


def _complex_linear_combine(e1, e2):
    a1r, a1i, b1r, b1i = e1
    a2r, a2i, b2r, b2i = e2
    return (a2r * a1r - a2i * a1i,
            a2r * a1i + a2i * a1r,
            a2r * b1r - a2i * b1i + b2r,
            a2r * b1i + a2i * b1r + b2i)


def s5_mixer(u, lam_re, lam_im, log_step, b_re, b_im, c_re, c_im, d, glu_w, glu_b):
    bsz, L, _ = u.shape
    f32 = jnp.float32
    uf = u.astype(f32).reshape(bsz, L, S5_GROUPS, S5_H)
    y = d.astype(f32).reshape(S5_GROUPS, S5_H) * uf
    for direction, rev in enumerate((False, True)):
        lr = lam_re[direction].astype(f32)
        li = lam_im[direction].astype(f32)
        step = jnp.exp(log_step[direction].astype(f32))[:, None]
        mag = jnp.exp(lr * step)
        ang = li * step
        ar = mag * jnp.cos(ang)
        ai = mag * jnp.sin(ang)
        den = lr * lr + li * li
        fr = ((ar - 1.0) * lr + ai * li) / den
        fi = (ai * lr - (ar - 1.0) * li) / den
        br = b_re[direction].astype(f32)
        bi = b_im[direction].astype(f32)
        bbar_re = fr[..., None] * br - fi[..., None] * bi
        bbar_im = fr[..., None] * bi + fi[..., None] * br
        bu_re = jnp.einsum('blgh,gph->blgp', uf, bbar_re)
        bu_im = jnp.einsum('blgh,gph->blgp', uf, bbar_im)
        a_re = jnp.broadcast_to(ar, (1, L, S5_GROUPS, S5_P))
        a_im = jnp.broadcast_to(ai, (1, L, S5_GROUPS, S5_P))
        _, _, s_re, s_im = lax.associative_scan(
            _complex_linear_combine, (a_re, a_im, bu_re, bu_im), reverse=rev, axis=1)
        y = y + jnp.einsum('blgp,ghp->blgh', s_re, c_re[direction].astype(f32)) \
              - jnp.einsum('blgp,ghp->blgh', s_im, c_im[direction].astype(f32))
    z = jax.nn.gelu(y.reshape(bsz, L, S5_WIDTH)).astype(u.dtype)
    return z * jax.nn.sigmoid(z @ glu_w + glu_b)


def conformer_conv(v, g, conv_w, conv_b, ln_g, ln_b):
    h = v * jax.nn.sigmoid(g)
    h = lax.conv_general_dilated(
        h, conv_w[:, None, :].astype(h.dtype), window_strides=(1,),
        padding=[(CONV_PAD, CONV_PAD)], dimension_numbers=('NWC', 'WIO', 'NWC'),
        feature_group_count=CONV_WIDTH) + conv_b
    h = layer_norm(h, ln_g, ln_b)
    return jax.nn.silu(h)


def diff_attention(q, k, v, lam):
    bsz, L = q.shape[0], q.shape[1]
    n_blk = L // Q_BLOCK
    qb = q.reshape(bsz, n_blk, Q_BLOCK, ATTN_HEADS, 2, ATTN_DH).transpose(1, 0, 2, 3, 4, 5)
    k_pos = jnp.arange(L)
    slopes = alibi_slopes()
    scale = ATTN_DH ** -0.5

    def block(args):
        q_blk, start = args
        s = jnp.einsum('bqhcd,bkhcd->bhcqk', q_blk, k).astype(jnp.float32) * scale
        q_pos = start + jnp.arange(Q_BLOCK)
        dist = jnp.abs(q_pos[:, None] - k_pos[None, :]).astype(jnp.float32)
        s = s - slopes[None, :, None, None, None] * dist
        p = jax.nn.softmax(s, axis=-1)
        w = p[:, :, 0] - lam * p[:, :, 1]
        return jnp.einsum('bhqk,bkhe->bqhe', w.astype(v.dtype), v)

    out = lax.map(block, (qb, jnp.arange(n_blk) * Q_BLOCK))
    return out.transpose(1, 0, 2, 3, 4).reshape(bsz, L, ATTN_HEADS, 2 * ATTN_DH)


def encoder_layer(x, l, p):
    bsz, L, _ = x.shape
    x = layer_norm(ALPHA * x + 0.5 * swiglu(x, p['ffn1_w_in'][l], p['ffn1_w_out'][l]),
                   p['ln_ffn1_g'][l], p['ln_ffn1_b'][l])
    proj = x @ p['w_in'][l]
    cuts = [S5_WIDTH, S5_WIDTH + CONV_WIDTH, S5_WIDTH + 2 * CONV_WIDTH,
            S5_WIDTH + 2 * CONV_WIDTH + ATTN_WIDTH, S5_WIDTH + 2 * CONV_WIDTH + 2 * ATTN_WIDTH]
    u_a, v_b, g_b, q, k, v = jnp.split(proj, cuts, axis=-1)
    y_a = s5_mixer(u_a, p['s5_lambda_re'][l], p['s5_lambda_im'][l], p['s5_log_step'][l],
                   p['s5_b_re'][l], p['s5_b_im'][l], p['s5_c_re'][l], p['s5_c_im'][l],
                   p['s5_d'][l], p['s5_glu_w'][l], p['s5_glu_b'][l])
    y_b = conformer_conv(v_b, g_b, p['conv_w'][l], p['conv_b'][l], p['conv_ln_g'][l], p['conv_ln_b'][l])
    lam_init = 0.8 - 0.6 * math.exp(-0.3 * l)
    f32 = jnp.float32
    lam = (jnp.exp(jnp.sum(p['attn_lambda_q1'][l].astype(f32) * p['attn_lambda_k1'][l].astype(f32)))
           - jnp.exp(jnp.sum(p['attn_lambda_q2'][l].astype(f32) * p['attn_lambda_k2'][l].astype(f32)))
           + lam_init)
    o = diff_attention(q.reshape(bsz, L, ATTN_HEADS, 2, ATTN_DH),
                       k.reshape(bsz, L, ATTN_HEADS, 2, ATTN_DH),
                       v.reshape(bsz, L, ATTN_HEADS, 2 * ATTN_DH), lam)
    y_c = (rms_norm(o, p['attn_subln_g'][l]) * (1.0 - lam_init)).reshape(bsz, L, ATTN_WIDTH)
    mix = jnp.concatenate([y_a, y_b, y_c], axis=-1) @ p['w_out'][l]
    x = layer_norm(ALPHA * x + mix, p['ln_mix_g'][l], p['ln_mix_b'][l])
    x = layer_norm(ALPHA * x + 0.5 * swiglu(x, p['ffn2_w_in'][l], p['ffn2_w_out'][l]),
                   p['ln_ffn2_g'][l], p['ln_ffn2_b'][l])
    return x


def trunk(x, p):
    for l in range(DEPTH):
        x = encoder_layer(x, l, p)
    return x


def setup_inputs(seed: int = 0) -> dict:
    key = jax.random.key(seed)
    ks = list(jax.random.split(key, 40))
    f32 = jnp.float32

    def nrm(shape, scale):
        return jax.random.normal(ks.pop(), shape, f32) * scale

    def gain(shape):
        return 1.0 + nrm(shape, 0.01)

    n_idx = jnp.arange(S5_P, dtype=f32)
    s5_shape = (DEPTH, 2, S5_GROUPS, S5_P)
    return {
        'x_prompt': nrm((BATCH, SEQ, D_MODEL), 1.0),
        'x_sample': nrm((DEC_BATCH, DEC_SEQ, D_MODEL), 1.0),
        'ffn1_w_in': nrm((DEPTH, D_MODEL, 2 * D_FF), D_MODEL ** -0.5),
        'ffn1_w_out': nrm((DEPTH, D_FF, D_MODEL), BETA * D_FF ** -0.5),
        'ln_ffn1_g': gain((DEPTH, D_MODEL)),
        'ln_ffn1_b': nrm((DEPTH, D_MODEL), 0.01),
        'w_in': nrm((DEPTH, D_MODEL, IN_COLS), D_MODEL ** -0.5),
        's5_lambda_re': -0.5 + nrm(s5_shape, 0.01),
        's5_lambda_im': math.pi * n_idx + nrm(s5_shape, 0.01),
        's5_log_step': jax.random.uniform(ks.pop(), (DEPTH, 2, S5_GROUPS), f32,
                                          math.log(1e-3), math.log(1e-1)),
        's5_b_re': nrm((DEPTH, 2, S5_GROUPS, S5_P, S5_H), (2 * S5_H) ** -0.5),
        's5_b_im': nrm((DEPTH, 2, S5_GROUPS, S5_P, S5_H), (2 * S5_H) ** -0.5),
        's5_c_re': nrm((DEPTH, 2, S5_GROUPS, S5_H, S5_P), (2 * S5_P) ** -0.5),
        's5_c_im': nrm((DEPTH, 2, S5_GROUPS, S5_H, S5_P), (2 * S5_P) ** -0.5),
        's5_d': nrm((DEPTH, S5_WIDTH), 1.0),
        's5_glu_w': nrm((DEPTH, S5_WIDTH, S5_WIDTH), S5_WIDTH ** -0.5),
        's5_glu_b': nrm((DEPTH, S5_WIDTH), 0.01),
        'conv_w': nrm((DEPTH, CONV_K, CONV_WIDTH), CONV_K ** -0.5),
        'conv_b': nrm((DEPTH, CONV_WIDTH), 0.01),
        'conv_ln_g': gain((DEPTH, CONV_WIDTH)),
        'conv_ln_b': nrm((DEPTH, CONV_WIDTH), 0.01),
        'attn_lambda_q1': nrm((DEPTH, ATTN_DH), 0.1),
        'attn_lambda_k1': nrm((DEPTH, ATTN_DH), 0.1),
        'attn_lambda_q2': nrm((DEPTH, ATTN_DH), 0.1),
        'attn_lambda_k2': nrm((DEPTH, ATTN_DH), 0.1),
        'attn_subln_g': gain((DEPTH, 2 * ATTN_DH)),
        'w_out': nrm((DEPTH, MIX_WIDTH, D_MODEL), BETA * MIX_WIDTH ** -0.5),
        'ln_mix_g': gain((DEPTH, D_MODEL)),
        'ln_mix_b': nrm((DEPTH, D_MODEL), 0.01),
        'ffn2_w_in': nrm((DEPTH, D_MODEL, 2 * D_FF), D_MODEL ** -0.5),
        'ffn2_w_out': nrm((DEPTH, D_FF, D_MODEL), BETA * D_FF ** -0.5),
        'ln_ffn2_g': gain((DEPTH, D_MODEL)),
        'ln_ffn2_b': nrm((DEPTH, D_MODEL), 0.01),
    }


def reference(x_prompt, x_sample, ffn1_w_in, ffn1_w_out, ln_ffn1_g, ln_ffn1_b, w_in,
              s5_lambda_re, s5_lambda_im, s5_log_step, s5_b_re, s5_b_im, s5_c_re, s5_c_im,
              s5_d, s5_glu_w, s5_glu_b, conv_w, conv_b, conv_ln_g, conv_ln_b,
              attn_lambda_q1, attn_lambda_k1, attn_lambda_q2, attn_lambda_k2, attn_subln_g,
              w_out, ln_mix_g, ln_mix_b, ffn2_w_in, ffn2_w_out, ln_ffn2_g, ln_ffn2_b):
    p = {
        'ffn1_w_in': ffn1_w_in, 'ffn1_w_out': ffn1_w_out,
        'ln_ffn1_g': ln_ffn1_g, 'ln_ffn1_b': ln_ffn1_b,
        'w_in': w_in,
        's5_lambda_re': s5_lambda_re, 's5_lambda_im': s5_lambda_im, 's5_log_step': s5_log_step,
        's5_b_re': s5_b_re, 's5_b_im': s5_b_im, 's5_c_re': s5_c_re, 's5_c_im': s5_c_im,
        's5_d': s5_d, 's5_glu_w': s5_glu_w, 's5_glu_b': s5_glu_b,
        'conv_w': conv_w, 'conv_b': conv_b, 'conv_ln_g': conv_ln_g, 'conv_ln_b': conv_ln_b,
        'attn_lambda_q1': attn_lambda_q1, 'attn_lambda_k1': attn_lambda_k1,
        'attn_lambda_q2': attn_lambda_q2, 'attn_lambda_k2': attn_lambda_k2,
        'attn_subln_g': attn_subln_g,
        'w_out': w_out, 'ln_mix_g': ln_mix_g, 'ln_mix_b': ln_mix_b,
        'ffn2_w_in': ffn2_w_in, 'ffn2_w_out': ffn2_w_out,
        'ln_ffn2_g': ln_ffn2_g, 'ln_ffn2_b': ln_ffn2_b,
    }
    y_prompt = trunk(x_prompt, p)
    y_sample = trunk(x_sample, p)
    return (y_prompt, y_sample)
```

```python
import functools
import math

import jax
import jax.numpy as jnp
from jax import lax
from jax.experimental import pallas as pl
from jax.experimental.pallas import tpu as pltpu

LN_EPS = 1e-5
S5_GROUPS_PER_BLOCK = 8
SUBLANES = 8
VMEM_LIMIT = 60 * 1024 * 1024

f32 = jnp.float32
bf16 = jnp.bfloat16


def _pick(n, candidates):
    for c in candidates:
        if n % c == 0:
            return c
    raise ValueError(f"no tile for {n} in {candidates}")


def _params(sem):
    return pltpu.CompilerParams(dimension_semantics=sem, vmem_limit_bytes=VMEM_LIMIT)


def _layer_norm(y, g, b):
    mu = jnp.mean(y, axis=-1, keepdims=True)
    yc = y - mu
    var = jnp.mean(yc * yc, axis=-1, keepdims=True)
    return yc * lax.rsqrt(var + LN_EPS) * g + b


def _ffn_kernel(x_ref, wg_ref, wu_ref, wo_ref, g_ref, b_ref, o_ref, obf_ref, xbf_ref, *, alpha):
    k = pl.program_id(1)

    @pl.when(k == 0)
    def _():
        xbf_ref[...] = x_ref[...].astype(bf16)
        o_ref[...] = jnp.zeros_like(o_ref)

    xb = xbf_ref[...]
    gate = jnp.dot(xb, wg_ref[...], preferred_element_type=f32)
    up = jnp.dot(xb, wu_ref[...], preferred_element_type=f32)
    h = (gate * jax.nn.sigmoid(gate) * up).astype(bf16)
    o_ref[...] += jnp.dot(h, wo_ref[...], preferred_element_type=f32)

    @pl.when(k == pl.num_programs(1) - 1)
    def _():
        y = _layer_norm(alpha * x_ref[...] + 0.5 * o_ref[...], g_ref[...], b_ref[...])
        o_ref[...] = y
        obf_ref[...] = y.astype(bf16)


def _ffn(x, w_in, w_out, g, b, alpha):
    n, d = x.shape
    dff = w_out.shape[0]
    tm = _pick(n, (512, 256, 128))
    tf = _pick(dff, (256, 128))
    nk = dff // tf
    return pl.pallas_call(
        functools.partial(_ffn_kernel, alpha=alpha),
        grid=(n // tm, nk),
        in_specs=[
            pl.BlockSpec((tm, d), lambda i, k: (i, 0), pipeline_mode=pl.Buffered(1)),
            pl.BlockSpec((d, tf), lambda i, k: (0, k)),
            pl.BlockSpec((d, tf), lambda i, k: (0, k + nk)),
            pl.BlockSpec((tf, d), lambda i, k: (k, 0)),
            pl.BlockSpec((1, d), lambda i, k: (0, 0)),
            pl.BlockSpec((1, d), lambda i, k: (0, 0)),
        ],
        out_specs=[
            pl.BlockSpec((tm, d), lambda i, k: (i, 0)),
            pl.BlockSpec((tm, d), lambda i, k: (i, 0)),
        ],
        out_shape=[jax.ShapeDtypeStruct((n, d), f32), jax.ShapeDtypeStruct((n, d), bf16)],
        scratch_shapes=[pltpu.VMEM((tm, d), bf16)],
        compiler_params=_params(("parallel", "arbitrary")),
        name="ffn",
    )(x, w_in, w_in, w_out, g, b)


def _matmul_kernel(x_ref, w_ref, o_ref):
    o_ref[...] = jnp.dot(x_ref[...], w_ref[...], preferred_element_type=f32).astype(o_ref.dtype)


def _matmul(x, w, col0, ncols, out_dtype):
    n, d = x.shape
    tm = _pick(n, (1024, 512, 256, 128))
    tn = _pick(math.gcd(ncols, col0) if col0 else ncols, (1024, 768, 512, 256, 128))
    off = col0 // tn
    return pl.pallas_call(
        _matmul_kernel,
        grid=(n // tm, ncols // tn),
        in_specs=[
            pl.BlockSpec((tm, d), lambda i, j: (i, 0)),
            pl.BlockSpec((d, tn), lambda i, j: (0, j + off)),
        ],
        out_specs=pl.BlockSpec((tm, tn), lambda i, j: (i, j)),
        out_shape=jax.ShapeDtypeStruct((n, ncols), out_dtype),
        compiler_params=_params(("parallel", "arbitrary")),
        name="in_proj",
    )(x, w)


def _s5_kernel(u_ref, bmat_ref, cmat_ref, a_ref, apow_ref, d_ref, o_ref,
               uperm_ref, s_ref, y_ref, *, seg):
    half = s_ref.shape[1] // 2

    def perm(i, c):
        uperm_ref[pl.ds(pl.multiple_of(i * SUBLANES, SUBLANES), SUBLANES), :] = (
            u_ref[pl.ds(i, SUBLANES, stride=seg), :])
        return c
    lax.fori_loop(0, seg, perm, 0)

    up = uperm_ref[...]
    ub = up.astype(bf16)
    y_ref[...] = d_ref[...] * up
    rowid = lax.broadcasted_iota(jnp.int32, (SUBLANES, half), 0)

    for dirn in (0, 1):
        rev = dirn == 1
        s_ref[...] = jnp.dot(ub, bmat_ref[dirn], preferred_element_type=f32)
        a_re = jnp.broadcast_to(a_ref[dirn, :, :half], (SUBLANES, half))
        a_im = jnp.broadcast_to(a_ref[dirn, :, half:], (SUBLANES, half))

        def local(n, carry, rev=rev, a_re=a_re, a_im=a_im):
            sr, si = carry
            i = seg - 1 - n if rev else n
            row = pl.multiple_of(i * SUBLANES, SUBLANES)
            nr = a_re * sr - a_im * si + s_ref[pl.ds(row, SUBLANES), :half]
            ni = a_re * si + a_im * sr + s_ref[pl.ds(row, SUBLANES), half:]
            s_ref[pl.ds(row, SUBLANES), :half] = nr
            s_ref[pl.ds(row, SUBLANES), half:] = ni
            return nr, ni

        zero = jnp.zeros((SUBLANES, half), f32)
        end_r, end_i = lax.fori_loop(0, seg, local, (zero, zero))

        p_re = jnp.broadcast_to(apow_ref[dirn, seg - 1:seg, :half], (SUBLANES, half))
        p_im = jnp.broadcast_to(apow_ref[dirn, seg - 1:seg, half:], (SUBLANES, half))
        x_re, x_im = zero, zero
        edge = SUBLANES - 1 if rev else 0
        shift = SUBLANES - 1 if rev else 1
        for _ in range(SUBLANES - 1):
            t_re = end_r + p_re * x_re - p_im * x_im
            t_im = end_i + p_re * x_im + p_im * x_re
            x_re = jnp.where(rowid == edge, 0.0, pltpu.roll(t_re, shift, 0))
            x_im = jnp.where(rowid == edge, 0.0, pltpu.roll(t_im, shift, 0))

        def fix(i, c, rev=rev, x_re=x_re, x_im=x_im):
            row = pl.multiple_of(i * SUBLANES, SUBLANES)
            pidx = seg - 1 - i if rev else i
            w_re = jnp.broadcast_to(apow_ref[dirn, pl.ds(pidx, 1), :half], (SUBLANES, half))
            w_im = jnp.broadcast_to(apow_ref[dirn, pl.ds(pidx, 1), half:], (SUBLANES, half))
            s_ref[pl.ds(row, SUBLANES), :half] += w_re * x_re - w_im * x_im
            s_ref[pl.ds(row, SUBLANES), half:] += w_re * x_im + w_im * x_re
            return c
        lax.fori_loop(0, seg, fix, 0)

        y_ref[...] += jnp.dot(s_ref[...].astype(bf16), cmat_ref[dirn], preferred_element_type=f32)

    y_ref[...] = jax.nn.gelu(y_ref[...])

    def unperm(i, c):
        o_ref[pl.ds(i, SUBLANES, stride=seg), :] = (
            y_ref[pl.ds(pl.multiple_of(i * SUBLANES, SUBLANES), SUBLANES), :])
        return c
    lax.fori_loop(0, seg, unperm, 0)


def _s5_tables(lam_re, lam_im, log_step, b_re, b_im, c_re, c_im, seg):
    _, g, p = lam_re.shape
    h = b_re.shape[-1]
    gb = S5_GROUPS_PER_BLOCK
    nb = g // gb
    lr, li = lam_re.astype(f32), lam_im.astype(f32)
    step = jnp.exp(log_step.astype(f32))[..., None]
    mag = jnp.exp(lr * step)
    ang = li * step
    ar, ai = mag * jnp.cos(ang), mag * jnp.sin(ang)
    den = lr * lr + li * li
    fr = ((ar - 1.0) * lr + ai * li) / den
    fi = (ai * lr - (ar - 1.0) * li) / den
    br, bi = b_re.astype(f32), b_im.astype(f32)
    bbar = jnp.stack([fr[..., None] * br - fi[..., None] * bi,
                      fr[..., None] * bi + fi[..., None] * br], axis=-1)
    eye = jnp.eye(gb, dtype=f32)
    bmat = jnp.einsum('djgphc,gk->djghckp', bbar.reshape(2, nb, gb, p, h, 2), eye)
    bmat = bmat.reshape(2, nb, gb * h, 2 * gb * p).astype(bf16)
    cc = jnp.stack([c_re.astype(f32), -c_im.astype(f32)], axis=-1)
    cmat = jnp.einsum('djghpc,gk->djcgpkh', cc.reshape(2, nb, gb, h, p, 2), eye)
    cmat = cmat.reshape(2, nb, 2 * gb * p, gb * h).astype(bf16)
    a = jnp.concatenate([ar.reshape(2, nb, 1, gb * p), ai.reshape(2, nb, 1, gb * p)], axis=-1)
    n = jnp.arange(1, seg + 1, dtype=f32)[None, :, None, None]
    pmag = jnp.exp(n * (lr * step)[:, None])
    pang = n * ang[:, None]
    pr = (pmag * jnp.cos(pang)).reshape(2, seg, nb, gb * p).transpose(0, 2, 1, 3)
    pi = (pmag * jnp.sin(pang)).reshape(2, seg, nb, gb * p).transpose(0, 2, 1, 3)
    apow = jnp.concatenate([pr, pi], axis=-1)
    return bmat, cmat, a, apow


def _s5(proj, nseq, seqlen, tables, d, width):
    bmat, cmat, a, apow = tables
    n = proj.shape[0]
    nb = bmat.shape[1]
    cw = bmat.shape[2]
    sw = bmat.shape[3]
    seg = seqlen // SUBLANES
    return pl.pallas_call(
        functools.partial(_s5_kernel, seg=seg),
        grid=(nseq, nb),
        in_specs=[
            pl.BlockSpec((seqlen, cw), lambda b, j: (b, j)),
            pl.BlockSpec((2, None, cw, sw), lambda b, j: (0, j, 0, 0)),
            pl.BlockSpec((2, None, sw, cw), lambda b, j: (0, j, 0, 0)),
            pl.BlockSpec((2, None, 1, sw), lambda b, j: (0, j, 0, 0)),
            pl.BlockSpec((2, None, seg, sw), lambda b, j: (0, j, 0, 0)),
            pl.BlockSpec((1, cw), lambda b, j: (0, j)),
        ],
        out_specs=pl.BlockSpec((seqlen, cw), lambda b, j: (b, j)),
        out_shape=jax.ShapeDtypeStruct((n, width), f32),
        scratch_shapes=[pltpu.VMEM((seqlen, cw), f32), pltpu.VMEM((seqlen, sw), f32),
                        pltpu.VMEM((seqlen, cw), f32)],
        compiler_params=_params(("parallel", "arbitrary")),
        name="s5_scan",
    )(proj, bmat, cmat, a, apow, d)


def _glu_kernel(z_ref, w_ref, b_ref, o_ref):
    z = z_ref[...]
    gate = jnp.dot(z.astype(bf16), w_ref[...], preferred_element_type=f32) + b_ref[...]
    o_ref[...] = (z * jax.nn.sigmoid(gate)).astype(o_ref.dtype)


def _glu(z, w, b):
    n, c = z.shape
    tm = _pick(n, (1024, 512, 256, 128))
    return pl.pallas_call(
        _glu_kernel,
        grid=(n // tm,),
        in_specs=[
            pl.BlockSpec((tm, c), lambda i: (i, 0)),
            pl.BlockSpec((c, c), lambda i: (0, 0)),
            pl.BlockSpec((1, c), lambda i: (0, 0)),
        ],
        out_specs=pl.BlockSpec((tm, c), lambda i: (i, 0)),
        out_shape=jax.ShapeDtypeStruct((n, c), bf16),
        compiler_params=_params(("parallel",)),
        name="s5_glu",
    )(z, w, b)


CONV_ROWS = 64
CONV_HALO = 16


def _conv_kernel(v_ref, g_ref, w_ref, b_ref, o_ref, hp_ref, *, taps, pad):
    seqlen, cw = v_ref.shape
    hp_ref[0:CONV_HALO, :] = jnp.zeros((CONV_HALO, cw), f32)
    hp_ref[seqlen + CONV_HALO:seqlen + 2 * CONV_HALO, :] = jnp.zeros((CONV_HALO, cw), f32)
    hp_ref[CONV_HALO:seqlen + CONV_HALO, :] = v_ref[...] * jax.nn.sigmoid(g_ref[...])
    bias = jnp.broadcast_to(b_ref[...], (CONV_ROWS, cw))

    def tile(i, c):
        base = pl.multiple_of(i * CONV_ROWS, CONV_ROWS)
        win = hp_ref[pl.ds(base, CONV_ROWS + 2 * CONV_HALO), :]
        acc = bias
        for k in range(taps):
            off = CONV_HALO - pad + k
            acc = acc + w_ref[k:k + 1, :] * win[off:off + CONV_ROWS, :]
        o_ref[pl.ds(base, CONV_ROWS), :] = acc
        return c
    lax.fori_loop(0, seqlen // CONV_ROWS, tile, 0)


def _conv(proj, nseq, seqlen, v_col0, g_col0, width, conv_w, conv_b):
    n = proj.shape[0]
    taps = conv_w.shape[0]
    cw = _pick(width, (256, 128))
    voff, goff = v_col0 // cw, g_col0 // cw
    return pl.pallas_call(
        functools.partial(_conv_kernel, taps=taps, pad=(taps - 1) // 2),
        grid=(nseq, width // cw),
        in_specs=[
            pl.BlockSpec((seqlen, cw), lambda b, j: (b, j + voff)),
            pl.BlockSpec((seqlen, cw), lambda b, j: (b, j + goff)),
            pl.BlockSpec((taps, cw), lambda b, j: (0, j)),
            pl.BlockSpec((1, cw), lambda b, j: (0, j)),
        ],
        out_specs=pl.BlockSpec((seqlen, cw), lambda b, j: (b, j)),
        out_shape=jax.ShapeDtypeStruct((n, width), f32),
        scratch_shapes=[pltpu.VMEM((seqlen + 2 * CONV_HALO, cw), f32)],
        compiler_params=_params(("parallel", "arbitrary")),
        name="conv",
    )(proj, proj, conv_w, conv_b)


def _ln_silu_kernel(x_ref, g_ref, b_ref, o_ref):
    y = _layer_norm(x_ref[...], g_ref[...], b_ref[...])
    o_ref[...] = (y * jax.nn.sigmoid(y)).astype(o_ref.dtype)


def _ln_silu(x, g, b):
    n, c = x.shape
    tm = _pick(n, (1024, 512, 256, 128))
    return pl.pallas_call(
        _ln_silu_kernel,
        grid=(n // tm,),
        in_specs=[
            pl.BlockSpec((tm, c), lambda i: (i, 0)),
            pl.BlockSpec((1, c), lambda i: (0, 0)),
            pl.BlockSpec((1, c), lambda i: (0, 0)),
        ],
        out_specs=pl.BlockSpec((tm, c), lambda i: (i, 0)),
        out_shape=jax.ShapeDtypeStruct((n, c), bf16),
        compiler_params=_params(("parallel",)),
        name="conv_ln",
    )(x, g, b)


def _attn_kernel(slopes_ref, lam_ref, q_ref, k_ref, v_ref, g_ref, o_ref, *, scale, post, dh):
    h = pl.program_id(1)
    i = pl.program_id(2)
    tq = q_ref.shape[0]
    seqlen = k_ref.shape[0]
    slope = slopes_ref[h]
    lam = lam_ref[0]
    rel = (lax.broadcasted_iota(jnp.int32, (tq, seqlen), 0)
           - lax.broadcasted_iota(jnp.int32, (tq, seqlen), 1))
    bias = slope * jnp.abs(rel + i * tq).astype(f32)
    probs = []
    for c in (0, 1):
        s = lax.dot_general(q_ref[:, c * dh:(c + 1) * dh], k_ref[:, c * dh:(c + 1) * dh],
                            (((1,), (1,)), ((), ())), preferred_element_type=f32)
        s = s * scale - bias
        e = jnp.exp(s - jnp.max(s, axis=-1, keepdims=True))
        probs.append(e * (1.0 / jnp.sum(e, axis=-1, keepdims=True)))
    w = (probs[0] - lam * probs[1]).astype(bf16)
    o = jnp.dot(w, v_ref[...], preferred_element_type=f32)
    ms = jnp.mean(o * o, axis=-1, keepdims=True)
    o_ref[...] = (o * lax.rsqrt(ms + LN_EPS) * g_ref[...] * post).astype(o_ref.dtype)


def _attn(qkv, nseq, seqlen, heads, dh, slopes, lam, g, post):
    n = qkv.shape[0]
    hw = 2 * dh
    tq = _pick(seqlen, (256, 128))
    nq = seqlen // tq
    return pl.pallas_call(
        functools.partial(_attn_kernel, scale=dh ** -0.5, post=post, dh=dh),
        grid=(nseq, heads, nq),
        in_specs=[
            pl.BlockSpec(memory_space=pltpu.SMEM),
            pl.BlockSpec(memory_space=pltpu.SMEM),
            pl.BlockSpec((tq, hw), lambda b, h, i: (b * nq + i, h)),
            pl.BlockSpec((seqlen, hw), lambda b, h, i: (b, heads + h)),
            pl.BlockSpec((seqlen, hw), lambda b, h, i: (b, 2 * heads + h)),
            pl.BlockSpec((1, hw), lambda b, h, i: (0, 0)),
        ],
        out_specs=pl.BlockSpec((tq, hw), lambda b, h, i: (b * nq + i, h)),
        out_shape=jax.ShapeDtypeStruct((n, heads * hw), bf16),
        compiler_params=_params(("parallel", "parallel", "arbitrary")),
        name="diff_attn",
    )(slopes, lam, qkv, qkv, qkv, g)


def _out_proj_kernel(a_ref, w_ref, x_ref, g_ref, b_ref, o_ref, obf_ref, *, alpha):
    k = pl.program_id(1)

    @pl.when(k == 0)
    def _():
        o_ref[...] = jnp.zeros_like(o_ref)

    o_ref[...] += jnp.dot(a_ref[...], w_ref[...], preferred_element_type=f32)

    @pl.when(k == pl.num_programs(1) - 1)
    def _():
        y = _layer_norm(alpha * x_ref[...] + o_ref[...], g_ref[...], b_ref[...])
        o_ref[...] = y
        obf_ref[...] = y.astype(bf16)


def _out_proj(a, w, x, g, b, alpha):
    n, kdim = a.shape
    d = w.shape[1]
    tm = _pick(n, (512, 256, 128))
    tk = _pick(kdim, (512, 256, 128))
    return pl.pallas_call(
        functools.partial(_out_proj_kernel, alpha=alpha),
        grid=(n // tm, kdim // tk),
        in_specs=[
            pl.BlockSpec((tm, tk), lambda i, k: (i, k)),
            pl.BlockSpec((tk, d), lambda i, k: (k, 0)),
            pl.BlockSpec((tm, d), lambda i, k: (i, 0), pipeline_mode=pl.Buffered(1)),
            pl.BlockSpec((1, d), lambda i, k: (0, 0)),
            pl.BlockSpec((1, d), lambda i, k: (0, 0)),
        ],
        out_specs=[
            pl.BlockSpec((tm, d), lambda i, k: (i, 0)),
            pl.BlockSpec((tm, d), lambda i, k: (i, 0)),
        ],
        out_shape=[jax.ShapeDtypeStruct((n, d), f32), jax.ShapeDtypeStruct((n, d), bf16)],
        compiler_params=_params(("parallel", "arbitrary")),
        name="out_proj",
    )(a, w, x, g, b)


def kernel(x_prompt, x_sample, ffn1_w_in, ffn1_w_out, ln_ffn1_g, ln_ffn1_b, w_in, s5_lambda_re, s5_lambda_im, s5_log_step, s5_b_re, s5_b_im, s5_c_re, s5_c_im, s5_d, s5_glu_w, s5_glu_b, conv_w, conv_b, conv_ln_g, conv_ln_b, attn_lambda_q1, attn_lambda_k1, attn_lambda_q2, attn_lambda_k2, attn_subln_g, w_out, ln_mix_g, ln_mix_b, ffn2_w_in, ffn2_w_out, ln_ffn2_g, ln_ffn2_b):
    depth = w_in.shape[0]
    bp, seqlen, d_model = x_prompt.shape
    bs, seqlen_s, _ = x_sample.shape
    assert seqlen == seqlen_s and seqlen % (SUBLANES * SUBLANES) == 0
    nseq = bp + bs
    alpha = (2 * depth) ** 0.25

    s5_w = s5_d.shape[1]
    conv_wd = conv_b.shape[1]
    dh = attn_lambda_q1.shape[1]
    attn_w = d_model - s5_w - conv_wd
    heads = attn_w // (2 * dh)
    f32_cols = s5_w + 2 * conv_wd
    seg = seqlen // SUBLANES

    x = jnp.concatenate([x_prompt.reshape(bp * seqlen, d_model),
                         x_sample.reshape(bs * seqlen, d_model)], axis=0).astype(f32)
    slopes = jnp.asarray([2.0 ** (-8.0 * (h + 1) / heads) for h in range(heads)], dtype=f32)
    row = lambda v: v.astype(f32).reshape(1, -1)

    for l in range(depth):
        x, xb = _ffn(x, ffn1_w_in[l].astype(bf16), ffn1_w_out[l].astype(bf16),
                     row(ln_ffn1_g[l]), row(ln_ffn1_b[l]), alpha)

        w_in_l = w_in[l].astype(bf16)
        proj = _matmul(xb, w_in_l, 0, f32_cols, f32)
        qkv = _matmul(xb, w_in_l, f32_cols, 3 * attn_w, bf16)

        tables = _s5_tables(s5_lambda_re[l], s5_lambda_im[l], s5_log_step[l], s5_b_re[l],
                            s5_b_im[l], s5_c_re[l], s5_c_im[l], seg)
        z = _s5(proj, nseq, seqlen, tables, row(s5_d[l]), s5_w)
        y_a = _glu(z, s5_glu_w[l].astype(bf16), row(s5_glu_b[l]))

        c = _conv(proj, nseq, seqlen, s5_w, s5_w + conv_wd, conv_wd,
                  conv_w[l].astype(f32), row(conv_b[l]))
        y_b = _ln_silu(c, row(conv_ln_g[l]), row(conv_ln_b[l]))

        lam_init = 0.8 - 0.6 * math.exp(-0.3 * l)
        lam = (jnp.exp(jnp.sum(attn_lambda_q1[l].astype(f32) * attn_lambda_k1[l].astype(f32)))
               - jnp.exp(jnp.sum(attn_lambda_q2[l].astype(f32) * attn_lambda_k2[l].astype(f32)))
               + lam_init).reshape(1)
        y_c = _attn(qkv, nseq, seqlen, heads, dh, slopes, lam, row(attn_subln_g[l]),
                    1.0 - lam_init)

        mix_in = jnp.concatenate([y_a, y_b, y_c], axis=-1)
        x, xb = _out_proj(mix_in, w_out[l].astype(bf16), x, row(ln_mix_g[l]), row(ln_mix_b[l]),
                          alpha)

        x, xb = _ffn(x, ffn2_w_in[l].astype(bf16), ffn2_w_out[l].astype(bf16),
                     row(ln_ffn2_g[l]), row(ln_ffn2_b[l]), alpha)

    y_prompt = x[:bp * seqlen].reshape(bp, seqlen, d_model).astype(x_prompt.dtype)
    y_sample = x[bp * seqlen:].reshape(bs, seqlen, d_model).astype(x_sample.dtype)
    return (y_prompt, y_sample)
```

```python
import functools
import math

import jax
import jax.numpy as jnp
from jax import lax
from jax.experimental import pallas as pl
from jax.experimental.pallas import tpu as pltpu

LN_EPS = 1e-5
S5_GROUPS_PER_BLOCK = 8
SUBLANES = 8
S5_SEG_PAD = 4
VMEM_LIMIT = 60 * 1024 * 1024
FFN_TF = 256
ATTN_ROWS = 128
LOG2E = 1.4426950408889634

f32 = jnp.float32
bf16 = jnp.bfloat16


def _pick(n, candidates):
    for c in candidates:
        if n % c == 0:
            return c
    raise ValueError(f"no tile for {n} in {candidates}")


def _params(sem):
    return pltpu.CompilerParams(dimension_semantics=sem, vmem_limit_bytes=VMEM_LIMIT)


def _layer_norm(y, g, b):
    mu = jnp.mean(y, axis=-1, keepdims=True)
    yc = y - mu
    var = jnp.mean(yc * yc, axis=-1, keepdims=True)
    return yc * lax.rsqrt(var + LN_EPS) * g + b


def _ffn_kernel(x_ref, wgu_ref, wo_ref, g_ref, b_ref, o_ref, *rest, alpha, tf):
    obf_ref = rest[0] if len(rest) == 2 else None
    xbf_ref = rest[-1]
    k = pl.program_id(1)

    @pl.when(k == 0)
    def _():
        xbf_ref[...] = x_ref[...].astype(bf16)
        o_ref[...] = jnp.zeros_like(o_ref)

    gu = jnp.dot(xbf_ref[...], wgu_ref[...], preferred_element_type=f32)
    gate, up = gu[:, :tf], gu[:, tf:]
    h = (gate * jax.nn.sigmoid(gate) * up).astype(bf16)
    o_ref[...] += jnp.dot(h, wo_ref[...], preferred_element_type=f32)

    @pl.when(k == pl.num_programs(1) - 1)
    def _():
        y = _layer_norm(alpha * x_ref[...] + 0.5 * o_ref[...], g_ref[...], b_ref[...])
        o_ref[...] = y
        if obf_ref is not None:
            obf_ref[...] = y.astype(bf16)


def _regroup_kernel(g_ref, u_ref, o_ref):
    tf = g_ref.shape[1]
    o_ref[:, :tf] = g_ref[...].astype(bf16)
    o_ref[:, tf:] = u_ref[...].astype(bf16)


def _ffn_weights(w_in, w_out):
    depth, d, two_dff = w_in.shape
    dff = two_dff // 2
    tf = _pick(dff, (FFN_TF, 128))
    nk = dff // tf
    wgu = pl.pallas_call(
        _regroup_kernel,
        grid=(depth, nk),
        in_specs=[
            pl.BlockSpec((None, d, tf), lambda l, k: (l, 0, k)),
            pl.BlockSpec((None, d, tf), lambda l, k: (l, 0, k + nk)),
        ],
        out_specs=pl.BlockSpec((None, None, d, 2 * tf), lambda l, k: (l, k, 0, 0)),
        out_shape=jax.ShapeDtypeStruct((depth, nk, d, 2 * tf), bf16),
        compiler_params=_params(("parallel", "parallel")),
        name="ffn_weight_regroup",
    )(w_in, w_in)
    return wgu, w_out.astype(bf16)


def _ffn(x, row0, nrows, wgu, wo, layer, g, b, alpha, want_bf16):
    d = x.shape[1]
    nk, tf = wgu.shape[1], wgu.shape[3] // 2
    tm = _pick(math.gcd(nrows, row0) if row0 else nrows, (512, 256, 128))
    roff = row0 // tm
    out_specs = [pl.BlockSpec((tm, d), lambda i, k: (i, 0))]
    out_shape = [jax.ShapeDtypeStruct((nrows, d), f32)]
    if want_bf16:
        out_specs.append(pl.BlockSpec((tm, d), lambda i, k: (i, 0)))
        out_shape.append(jax.ShapeDtypeStruct((nrows, d), bf16))
    return pl.pallas_call(
        functools.partial(_ffn_kernel, alpha=alpha, tf=tf),
        grid=(nrows // tm, nk),
        in_specs=[
            pl.BlockSpec((tm, d), lambda i, k: (i + roff, 0), pipeline_mode=pl.Buffered(1)),
            pl.BlockSpec((None, None, d, 2 * tf), lambda i, k: (layer, k, 0, 0)),
            pl.BlockSpec((None, tf, d), lambda i, k: (layer, k, 0)),
            pl.BlockSpec((1, d), lambda i, k: (0, 0)),
            pl.BlockSpec((1, d), lambda i, k: (0, 0)),
        ],
        out_specs=out_specs,
        out_shape=out_shape,
        scratch_shapes=[pltpu.VMEM((tm, d), bf16)],
        compiler_params=_params(("parallel", "arbitrary")),
        name="ffn",
    )(x, wgu, wo, g, b)


def _matmul_kernel(x_ref, w_ref, o_ref):
    o_ref[...] = jnp.dot(x_ref[...], w_ref[...], preferred_element_type=f32).astype(o_ref.dtype)


def _matmul(x, w, layer, col0, ncols, out_dtype):
    n, d = x.shape
    tm = _pick(n, (1024, 512, 256, 128))
    tn = _pick(math.gcd(ncols, col0) if col0 else ncols, (1024, 768, 512, 256, 128))
    off = col0 // tn
    return pl.pallas_call(
        _matmul_kernel,
        grid=(n // tm, ncols // tn),
        in_specs=[
            pl.BlockSpec((tm, d), lambda i, j: (i, 0)),
            pl.BlockSpec((None, d, tn), lambda i, j: (layer, 0, j + off)),
        ],
        out_specs=pl.BlockSpec((tm, tn), lambda i, j: (i, j)),
        out_shape=jax.ShapeDtypeStruct((n, ncols), out_dtype),
        compiler_params=_params(("parallel", "arbitrary")),
        name="in_proj",
    )(x, w)


def _s5_kernel(u_ref, bmat_ref, cmat_ref, a_ref, aseg_ref, d_ref, o_ref,
               stage_ref, perm_ref, s_ref, y_ref, *, seg):
    seqlen, cw = u_ref.shape
    padded = seg * SUBLANES
    half = s_ref.shape[1] // 2

    stage_ref[0:seqlen, :] = u_ref[...]
    stage_ref[seqlen:padded, :] = jnp.zeros((padded - seqlen, cw), f32)

    def perm(i, c):
        perm_ref[pl.ds(pl.multiple_of(i * SUBLANES, SUBLANES), SUBLANES), :] = (
            stage_ref[pl.ds(i, SUBLANES, stride=seg), :])
        return c
    lax.fori_loop(0, seg, perm, 0)

    up = perm_ref[...]
    ub = up.astype(bf16)
    y_ref[...] = d_ref[...] * up
    rowid = lax.broadcasted_iota(jnp.int32, (SUBLANES, half), 0)
    zero = jnp.zeros((SUBLANES, half), f32)

    for dirn in (0, 1):
        rev = dirn == 1
        s_ref[...] = jnp.dot(ub, bmat_ref[dirn], preferred_element_type=f32)
        a_re, a_im = a_ref[dirn, :, :half], a_ref[dirn, :, half:]

        def scan(n, carry, store, rev=rev, a_re=a_re, a_im=a_im):
            sr, si = carry
            i = seg - 1 - n if rev else n
            row = pl.multiple_of(i * SUBLANES, SUBLANES)
            nr = a_re * sr - a_im * si + s_ref[pl.ds(row, SUBLANES), :half]
            ni = a_re * si + a_im * sr + s_ref[pl.ds(row, SUBLANES), half:]
            if store:
                s_ref[pl.ds(row, SUBLANES), :half] = nr
                s_ref[pl.ds(row, SUBLANES), half:] = ni
            return nr, ni

        end_r, end_i = lax.fori_loop(0, seg, functools.partial(scan, store=False),
                                     (zero, zero), unroll=2)

        p_re, p_im = aseg_ref[dirn, :, :half], aseg_ref[dirn, :, half:]
        x_re, x_im = zero, zero
        edge = SUBLANES - 1 if rev else 0
        shift = SUBLANES - 1 if rev else 1
        for _ in range(SUBLANES - 1):
            t_re = end_r + p_re * x_re - p_im * x_im
            t_im = end_i + p_re * x_im + p_im * x_re
            x_re = jnp.where(rowid == edge, 0.0, pltpu.roll(t_re, shift, 0))
            x_im = jnp.where(rowid == edge, 0.0, pltpu.roll(t_im, shift, 0))

        lax.fori_loop(0, seg, functools.partial(scan, store=True), (x_re, x_im), unroll=2)

        y_ref[...] += jnp.dot(s_ref[...].astype(bf16), cmat_ref[dirn], preferred_element_type=f32)

    y_ref[...] = jax.nn.gelu(y_ref[...])

    def unperm(i, c):
        stage_ref[pl.ds(i, SUBLANES, stride=seg), :] = (
            y_ref[pl.ds(pl.multiple_of(i * SUBLANES, SUBLANES), SUBLANES), :])
        return c
    lax.fori_loop(0, seg, unperm, 0)
    o_ref[...] = stage_ref[0:seqlen, :]


def _s5_tables(lam_re, lam_im, log_step, b_re, b_im, c_re, c_im, seg):
    depth, _, g, p = lam_re.shape
    h = b_re.shape[-1]
    gb = S5_GROUPS_PER_BLOCK
    nb = g // gb
    lr, li = lam_re.astype(f32), lam_im.astype(f32)
    step = jnp.exp(log_step.astype(f32))[..., None]
    mag = jnp.exp(lr * step)
    ang = li * step
    ar, ai = mag * jnp.cos(ang), mag * jnp.sin(ang)
    den = lr * lr + li * li
    fr = ((ar - 1.0) * lr + ai * li) / den
    fi = (ai * lr - (ar - 1.0) * li) / den
    br, bi = b_re.astype(f32), b_im.astype(f32)
    bbar = jnp.stack([fr[..., None] * br - fi[..., None] * bi,
                      fr[..., None] * bi + fi[..., None] * br], axis=-1)
    eye = jnp.eye(gb, dtype=f32)
    bmat = jnp.einsum('ldjgphc,gk->ldjghckp', bbar.reshape(depth, 2, nb, gb, p, h, 2), eye)
    bmat = bmat.reshape(depth, 2, nb, gb * h, 2 * gb * p).astype(bf16)
    cc = jnp.stack([c_re.astype(f32), -c_im.astype(f32)], axis=-1)
    cmat = jnp.einsum('ldjghpc,gk->ldjcgpkh', cc.reshape(depth, 2, nb, gb, h, p, 2), eye)
    cmat = cmat.reshape(depth, 2, nb, 2 * gb * p, gb * h).astype(bf16)

    def rows(re, im):
        v = jnp.concatenate([re.reshape(depth, 2, nb, 1, gb * p), im.reshape(depth, 2, nb, 1, gb * p)],
                            axis=-1)
        return jnp.broadcast_to(v, (depth, 2, nb, SUBLANES, 2 * gb * p))

    a = rows(ar, ai)
    smag = jnp.exp(seg * (lr * step))
    aseg = rows(smag * jnp.cos(seg * ang), smag * jnp.sin(seg * ang))
    return bmat, cmat, a, aseg


def _s5(proj, nseq, seqlen, tables, layer, d, width):
    bmat, cmat, a, aseg = tables
    n = proj.shape[0]
    nb, cw, sw = bmat.shape[2], bmat.shape[3], bmat.shape[4]
    seg = seqlen // SUBLANES + S5_SEG_PAD
    padded = seg * SUBLANES
    return pl.pallas_call(
        functools.partial(_s5_kernel, seg=seg),
        grid=(nseq, nb),
        in_specs=[
            pl.BlockSpec((seqlen, cw), lambda b, j: (b, j)),
            pl.BlockSpec((None, 2, None, cw, sw), lambda b, j: (layer, 0, j, 0, 0)),
            pl.BlockSpec((None, 2, None, sw, cw), lambda b, j: (layer, 0, j, 0, 0)),
            pl.BlockSpec((None, 2, None, SUBLANES, sw), lambda b, j: (layer, 0, j, 0, 0)),
            pl.BlockSpec((None, 2, None, SUBLANES, sw), lambda b, j: (layer, 0, j, 0, 0)),
            pl.BlockSpec((1, cw), lambda b, j: (0, j)),
        ],
        out_specs=pl.BlockSpec((seqlen, cw), lambda b, j: (b, j)),
        out_shape=jax.ShapeDtypeStruct((n, width), f32),
        scratch_shapes=[pltpu.VMEM((padded, cw), f32), pltpu.VMEM((padded, cw), f32),
                        pltpu.VMEM((padded, sw), f32), pltpu.VMEM((padded, cw), f32)],
        compiler_params=_params(("parallel", "arbitrary")),
        name="s5_scan",
    )(proj, bmat, cmat, a, aseg, d)


def _glu_kernel(z_ref, w_ref, b_ref, o_ref):
    z = z_ref[...]
    gate = jnp.dot(z.astype(bf16), w_ref[...], preferred_element_type=f32) + b_ref[...]
    o_ref[...] = (z * jax.nn.sigmoid(gate)).astype(o_ref.dtype)


def _glu(z, w, layer, b):
    n, c = z.shape
    tm = _pick(n, (1024, 512, 256, 128))
    return pl.pallas_call(
        _glu_kernel,
        grid=(n // tm,),
        in_specs=[
            pl.BlockSpec((tm, c), lambda i: (i, 0)),
            pl.BlockSpec((None, c, c), lambda i: (layer, 0, 0)),
            pl.BlockSpec((1, c), lambda i: (0, 0)),
        ],
        out_specs=pl.BlockSpec((tm, c), lambda i: (i, 0)),
        out_shape=jax.ShapeDtypeStruct((n, c), bf16),
        compiler_params=_params(("parallel",)),
        name="s5_glu",
    )(z, w, b)


CONV_ROWS = 64
CONV_HALO = 16


def _conv_kernel(v_ref, g_ref, w_ref, b_ref, o_ref, hp_ref, *, taps, pad):
    seqlen, cw = v_ref.shape
    hp_ref[0:CONV_HALO, :] = jnp.zeros((CONV_HALO, cw), f32)
    hp_ref[seqlen + CONV_HALO:seqlen + 2 * CONV_HALO, :] = jnp.zeros((CONV_HALO, cw), f32)
    hp_ref[CONV_HALO:seqlen + CONV_HALO, :] = v_ref[...] * jax.nn.sigmoid(g_ref[...])
    bias = jnp.broadcast_to(b_ref[...], (CONV_ROWS, cw))
    first = CONV_HALO - pad
    span = (first + taps - 1) // SUBLANES * SUBLANES + CONV_ROWS

    def tile(i, c):
        base = pl.multiple_of(i * CONV_ROWS, CONV_ROWS)
        win = hp_ref[pl.ds(base, CONV_ROWS + 2 * CONV_HALO), :]
        acc = bias
        for phase in range(SUBLANES):
            ks = [k for k in range(taps) if (first + k) % SUBLANES == phase]
            if not ks:
                continue
            shifted = win[phase:phase + span, :]
            for k in ks:
                q = (first + k) // SUBLANES * SUBLANES
                acc = acc + w_ref[k:k + 1, :] * shifted[q:q + CONV_ROWS, :]
        o_ref[pl.ds(base, CONV_ROWS), :] = acc
        return c
    lax.fori_loop(0, seqlen // CONV_ROWS, tile, 0)


def _conv(proj, nseq, seqlen, v_col0, g_col0, width, conv_w, layer, conv_b):
    n = proj.shape[0]
    taps = conv_w.shape[1]
    pad = (taps - 1) // 2
    assert pad <= CONV_HALO and CONV_HALO - pad + taps - 1 + CONV_ROWS <= CONV_ROWS + 2 * CONV_HALO
    cw = _pick(width, (256, 128))
    voff, goff = v_col0 // cw, g_col0 // cw
    return pl.pallas_call(
        functools.partial(_conv_kernel, taps=taps, pad=pad),
        grid=(nseq, width // cw),
        in_specs=[
            pl.BlockSpec((seqlen, cw), lambda b, j: (b, j + voff)),
            pl.BlockSpec((seqlen, cw), lambda b, j: (b, j + goff)),
            pl.BlockSpec((None, taps, cw), lambda b, j: (layer, 0, j)),
            pl.BlockSpec((1, cw), lambda b, j: (0, j)),
        ],
        out_specs=pl.BlockSpec((seqlen, cw), lambda b, j: (b, j)),
        out_shape=jax.ShapeDtypeStruct((n, width), f32),
        scratch_shapes=[pltpu.VMEM((seqlen + 2 * CONV_HALO, cw), f32)],
        compiler_params=_params(("parallel", "arbitrary")),
        name="conv",
    )(proj, proj, conv_w, conv_b)


def _ln_silu_kernel(x_ref, g_ref, b_ref, o_ref):
    y = _layer_norm(x_ref[...], g_ref[...], b_ref[...])
    o_ref[...] = (y * jax.nn.sigmoid(y)).astype(o_ref.dtype)


def _ln_silu(x, g, b):
    n, c = x.shape
    tm = _pick(n, (1024, 512, 256, 128))
    return pl.pallas_call(
        _ln_silu_kernel,
        grid=(n // tm,),
        in_specs=[
            pl.BlockSpec((tm, c), lambda i: (i, 0)),
            pl.BlockSpec((1, c), lambda i: (0, 0)),
            pl.BlockSpec((1, c), lambda i: (0, 0)),
        ],
        out_specs=pl.BlockSpec((tm, c), lambda i: (i, 0)),
        out_shape=jax.ShapeDtypeStruct((n, c), bf16),
        compiler_params=_params(("parallel",)),
        name="conv_ln",
    )(x, g, b)


def _attn_kernel(slopes_ref, lam_ref, q_ref, k_ref, v_ref, g_ref, o_ref, bias_ref, *, scale, post, dh):
    h = pl.program_id(0)
    b = pl.program_id(1)
    i = pl.program_id(2)
    tq = q_ref.shape[0]
    seqlen = k_ref.shape[0]
    lam = lam_ref[0]

    @pl.when(b == 0)
    def _():
        rel = (lax.broadcasted_iota(jnp.int32, (tq, seqlen), 0)
               - lax.broadcasted_iota(jnp.int32, (tq, seqlen), 1))
        bias_ref[i] = (slopes_ref[h] * LOG2E) * jnp.abs(rel + i * tq).astype(f32)

    for r in range(tq // ATTN_ROWS):
        rows = slice(r * ATTN_ROWS, (r + 1) * ATTN_ROWS)
        bias = bias_ref[i, rows, :]
        es, rs = [], []
        for c in (0, 1):
            s = lax.dot_general(q_ref[rows, c * dh:(c + 1) * dh], k_ref[:, c * dh:(c + 1) * dh],
                                (((1,), (1,)), ((), ())), preferred_element_type=f32)
            s = s * (scale * LOG2E) - bias
            e = jnp.exp2(s - jnp.max(s, axis=-1, keepdims=True))
            es.append(e)
            rs.append(1.0 / jnp.sum(e, axis=-1, keepdims=True))
        w = (es[0] * rs[0] - es[1] * (lam * rs[1])).astype(bf16)
        o = jnp.dot(w, v_ref[...], preferred_element_type=f32)
        ms = jnp.mean(o * o, axis=-1, keepdims=True)
        o_ref[rows, :] = (o * lax.rsqrt(ms + LN_EPS) * g_ref[...] * post).astype(o_ref.dtype)


def _attn(qkv, nseq, seqlen, heads, dh, slopes, lam, g, post):
    n = qkv.shape[0]
    hw = 2 * dh
    tq = _pick(seqlen, (512, 256, 128))
    nq = seqlen // tq
    return pl.pallas_call(
        functools.partial(_attn_kernel, scale=dh ** -0.5, post=post, dh=dh),
        grid=(heads, nseq, nq),
        in_specs=[
            pl.BlockSpec(memory_space=pltpu.SMEM),
            pl.BlockSpec(memory_space=pltpu.SMEM),
            pl.BlockSpec((tq, hw), lambda h, b, i: (b * nq + i, h)),
            pl.BlockSpec((seqlen, hw), lambda h, b, i: (b, heads + h)),
            pl.BlockSpec((seqlen, hw), lambda h, b, i: (b, 2 * heads + h)),
            pl.BlockSpec((1, hw), lambda h, b, i: (0, 0)),
        ],
        out_specs=pl.BlockSpec((tq, hw), lambda h, b, i: (b * nq + i, h)),
        out_shape=jax.ShapeDtypeStruct((n, heads * hw), bf16),
        scratch_shapes=[pltpu.VMEM((nq, tq, seqlen), f32)],
        compiler_params=_params(("arbitrary", "arbitrary", "arbitrary")),
        name="diff_attn",
    )(slopes, lam, qkv, qkv, qkv, g)


def _out_proj_kernel(ya_ref, yb_ref, yc_ref, w_ref, x_ref, g_ref, b_ref, o_ref, *, alpha):
    ca, cb = ya_ref.shape[1], yb_ref.shape[1]
    mix = jnp.dot(ya_ref[...], w_ref[0:ca, :], preferred_element_type=f32)
    mix += jnp.dot(yb_ref[...], w_ref[ca:ca + cb, :], preferred_element_type=f32)
    mix += jnp.dot(yc_ref[...], w_ref[ca + cb:, :], preferred_element_type=f32)
    o_ref[...] = _layer_norm(alpha * x_ref[...] + mix, g_ref[...], b_ref[...])


def _out_proj(ya, yb, yc, w, layer, x, g, b, alpha):
    n, d = x.shape
    tm = _pick(n, (256, 128))
    row_spec = lambda a: pl.BlockSpec((tm, a.shape[1]), lambda i: (i, 0))
    return pl.pallas_call(
        functools.partial(_out_proj_kernel, alpha=alpha),
        grid=(n // tm,),
        in_specs=[
            row_spec(ya), row_spec(yb), row_spec(yc),
            pl.BlockSpec((None,) + w.shape[1:], lambda i: (layer, 0, 0), pipeline_mode=pl.Buffered(1)),
            row_spec(x),
            pl.BlockSpec((1, d), lambda i: (0, 0)),
            pl.BlockSpec((1, d), lambda i: (0, 0)),
        ],
        out_specs=pl.BlockSpec((tm, d), lambda i: (i, 0)),
        out_shape=jax.ShapeDtypeStruct((n, d), f32),
        compiler_params=_params(("parallel",)),
        name="out_proj",
    )(ya, yb, yc, w, x, g, b)


def kernel(x_prompt, x_sample, ffn1_w_in, ffn1_w_out, ln_ffn1_g, ln_ffn1_b, w_in, s5_lambda_re, s5_lambda_im, s5_log_step, s5_b_re, s5_b_im, s5_c_re, s5_c_im, s5_d, s5_glu_w, s5_glu_b, conv_w, conv_b, conv_ln_g, conv_ln_b, attn_lambda_q1, attn_lambda_k1, attn_lambda_q2, attn_lambda_k2, attn_subln_g, w_out, ln_mix_g, ln_mix_b, ffn2_w_in, ffn2_w_out, ln_ffn2_g, ln_ffn2_b):
    depth = w_in.shape[0]
    bp, seqlen, d_model = x_prompt.shape
    bs, seqlen_s, _ = x_sample.shape
    assert seqlen == seqlen_s and seqlen % (SUBLANES * SUBLANES) == 0
    nseq = bp + bs
    n_p, n_s = bp * seqlen, bs * seqlen
    n = n_p + n_s
    alpha = (2 * depth) ** 0.25

    s5_w = s5_d.shape[1]
    conv_wd = conv_b.shape[1]
    dh = attn_lambda_q1.shape[1]
    attn_w = d_model - s5_w - conv_wd
    heads = attn_w // (2 * dh)
    f32_cols = s5_w + 2 * conv_wd
    seg = seqlen // SUBLANES + S5_SEG_PAD

    x = jnp.concatenate([x_prompt.reshape(n_p, d_model), x_sample.reshape(n_s, d_model)],
                        axis=0).astype(f32)
    slopes = jnp.asarray([2.0 ** (-8.0 * (h + 1) / heads) for h in range(heads)], dtype=f32)
    row = lambda v: v.astype(f32).reshape(1, -1)

    ffn1_wgu, ffn1_wo = _ffn_weights(ffn1_w_in, ffn1_w_out)
    ffn2_wgu, ffn2_wo = _ffn_weights(ffn2_w_in, ffn2_w_out)
    w_in_b, w_out_b, glu_w_b = w_in.astype(bf16), w_out.astype(bf16), s5_glu_w.astype(bf16)
    conv_w_f = conv_w.astype(f32)
    tables = _s5_tables(s5_lambda_re, s5_lambda_im, s5_log_step, s5_b_re, s5_b_im, s5_c_re,
                        s5_c_im, seg)

    for l in range(depth):
        x, xb = _ffn(x, 0, n, ffn1_wgu, ffn1_wo, l, row(ln_ffn1_g[l]), row(ln_ffn1_b[l]), alpha, True)

        proj = _matmul(xb, w_in_b, l, 0, f32_cols, f32)
        qkv = _matmul(xb, w_in_b, l, f32_cols, 3 * attn_w, bf16)

        z = _s5(proj, nseq, seqlen, tables, l, row(s5_d[l]), s5_w)
        y_a = _glu(z, glu_w_b, l, row(s5_glu_b[l]))

        c = _conv(proj, nseq, seqlen, s5_w, s5_w + conv_wd, conv_wd, conv_w_f, l, row(conv_b[l]))
        y_b = _ln_silu(c, row(conv_ln_g[l]), row(conv_ln_b[l]))

        lam_init = 0.8 - 0.6 * math.exp(-0.3 * l)
        lam = (jnp.exp(jnp.sum(attn_lambda_q1[l].astype(f32) * attn_lambda_k1[l].astype(f32)))
               - jnp.exp(jnp.sum(attn_lambda_q2[l].astype(f32) * attn_lambda_k2[l].astype(f32)))
               + lam_init).reshape(1)
        y_c = _attn(qkv, nseq, seqlen, heads, dh, slopes, lam, row(attn_subln_g[l]), 1.0 - lam_init)

        x = _out_proj(y_a, y_b, y_c, w_out_b, l, x, row(ln_mix_g[l]), row(ln_mix_b[l]), alpha)

        g2, b2 = row(ln_ffn2_g[l]), row(ln_ffn2_b[l])
        if l + 1 < depth:
            x, = _ffn(x, 0, n, ffn2_wgu, ffn2_wo, l, g2, b2, alpha, False)
        else:
            y_p, = _ffn(x, 0, n_p, ffn2_wgu, ffn2_wo, l, g2, b2, alpha, False)
            y_s, = _ffn(x, n_p, n_s, ffn2_wgu, ffn2_wo, l, g2, b2, alpha, False)

    y_prompt = y_p.reshape(bp, seqlen, d_model).astype(x_prompt.dtype)
    y_sample = y_s.reshape(bs, seqlen, d_model).astype(x_sample.dtype)
    return (y_prompt, y_sample)
```

```python
import functools
import math

import jax
import jax.numpy as jnp
from jax import lax
from jax.experimental import pallas as pl
from jax.experimental.pallas import tpu as pltpu

LN_EPS = 1e-5
S5_GROUPS_PER_BLOCK = 16
SUBLANES = 8
LANES = 128
S5_SEG_PAD = 4
VMEM_LIMIT = 60 * 1024 * 1024
FFN_TF = 256
ATTN_ROWS = 256
LOG2E = 1.4426950408889634

f32 = jnp.float32
bf16 = jnp.bfloat16


def _pick(n, candidates):
    for c in candidates:
        if n % c == 0:
            return c
    raise ValueError(f"no tile for {n} in {candidates}")


def _params(sem):
    return pltpu.CompilerParams(dimension_semantics=sem, vmem_limit_bytes=VMEM_LIMIT)


def _layer_norm(y, g, b, eps=LN_EPS):
    mu = jnp.mean(y, axis=-1, keepdims=True)
    yc = y - mu
    var = jnp.mean(yc * yc, axis=-1, keepdims=True)
    return yc * lax.rsqrt(var + eps) * g + b


def _ffn_kernel(x_ref, wgu_ref, wo_ref, g_ref, b_ref, o_ref, *rest, alpha, tf):
    obf_ref = rest[0] if len(rest) == 2 else None
    xbf_ref = rest[-1]
    k = pl.program_id(1)

    @pl.when(k == 0)
    def _():
        x = x_ref[...]
        xbf_ref[...] = x.astype(bf16)
        o_ref[...] = (2.0 * alpha) * x

    gu = jnp.dot(xbf_ref[...], wgu_ref[...], preferred_element_type=f32)
    gate, up = gu[:, :tf], gu[:, tf:]
    h = (gate * jax.nn.sigmoid(gate) * up).astype(bf16)
    o_ref[...] += jnp.dot(h, wo_ref[...], preferred_element_type=f32)

    @pl.when(k == pl.num_programs(1) - 1)
    def _():
        y = _layer_norm(o_ref[...], g_ref[...], b_ref[...], eps=4.0 * LN_EPS)
        o_ref[...] = y
        if obf_ref is not None:
            obf_ref[...] = y.astype(bf16)


def _regroup_kernel(g_ref, u_ref, o_ref):
    tf = g_ref.shape[1]
    o_ref[:, :tf] = g_ref[...].astype(bf16)
    o_ref[:, tf:] = u_ref[...].astype(bf16)


def _ffn_weights(w_in, w_out):
    depth, d, two_dff = w_in.shape
    dff = two_dff // 2
    tf = _pick(dff, (FFN_TF, 128))
    nk = dff // tf
    wgu = pl.pallas_call(
        _regroup_kernel,
        grid=(depth, nk),
        in_specs=[
            pl.BlockSpec((None, d, tf), lambda l, k: (l, 0, k)),
            pl.BlockSpec((None, d, tf), lambda l, k: (l, 0, k + nk)),
        ],
        out_specs=pl.BlockSpec((None, None, d, 2 * tf), lambda l, k: (l, k, 0, 0)),
        out_shape=jax.ShapeDtypeStruct((depth, nk, d, 2 * tf), bf16),
        compiler_params=_params(("parallel", "parallel")),
        name="ffn_weight_regroup",
    )(w_in, w_in)
    return wgu, w_out.astype(bf16)


def _ffn(x, row0, nrows, wgu, wo, layer, g, b, alpha, want_bf16):
    d = x.shape[1]
    nk, tf = wgu.shape[1], wgu.shape[3] // 2
    tm = _pick(math.gcd(nrows, row0) if row0 else nrows, (512, 256, 128))
    roff = row0 // tm
    out_specs = [pl.BlockSpec((tm, d), lambda i, k: (i, 0))]
    out_shape = [jax.ShapeDtypeStruct((nrows, d), f32)]
    if want_bf16:
        out_specs.append(pl.BlockSpec((tm, d), lambda i, k: (i, 0)))
        out_shape.append(jax.ShapeDtypeStruct((nrows, d), bf16))
    return pl.pallas_call(
        functools.partial(_ffn_kernel, alpha=alpha, tf=tf),
        grid=(nrows // tm, nk),
        in_specs=[
            pl.BlockSpec((tm, d), lambda i, k: (i + roff, 0), pipeline_mode=pl.Buffered(1)),
            pl.BlockSpec((None, None, d, 2 * tf), lambda i, k: (layer, k, 0, 0)),
            pl.BlockSpec((None, tf, d), lambda i, k: (layer, k, 0)),
            pl.BlockSpec((1, d), lambda i, k: (0, 0)),
            pl.BlockSpec((1, d), lambda i, k: (0, 0)),
        ],
        out_specs=out_specs,
        out_shape=out_shape,
        scratch_shapes=[pltpu.VMEM((tm, d), bf16)],
        compiler_params=_params(("parallel", "arbitrary")),
        name="ffn",
    )(x, wgu, wo, g, b)


def _matmul_kernel(x_ref, w_ref, o_ref):
    o_ref[...] = jnp.dot(x_ref[...], w_ref[...], preferred_element_type=f32).astype(o_ref.dtype)


def _matmul(x, w, layer, col0, ncols, out_dtype):
    n, d = x.shape
    tm = _pick(n, (1024, 512, 256, 128))
    tn = _pick(math.gcd(ncols, col0) if col0 else ncols, (1024, 768, 512, 256, 128))
    off = col0 // tn
    return pl.pallas_call(
        _matmul_kernel,
        grid=(n // tm, ncols // tn),
        in_specs=[
            pl.BlockSpec((tm, d), lambda i, j: (i, 0)),
            pl.BlockSpec((None, d, tn), lambda i, j: (layer, 0, j + off)),
        ],
        out_specs=pl.BlockSpec((tm, tn), lambda i, j: (i, j)),
        out_shape=jax.ShapeDtypeStruct((n, ncols), out_dtype),
        compiler_params=_params(("parallel", "arbitrary")),
        name="in_proj",
    )(x, w)


def _s5_kernel(u_ref, bmat_ref, cmat_ref, a_ref, aseg_ref, d_ref, o_ref,
               stage_ref, perm_ref, s_ref, y_ref, *, seg):
    seqlen, cw = u_ref.shape
    padded = seg * SUBLANES
    half = s_ref.shape[1] // 2
    slabs = [slice(sl * LANES, (sl + 1) * LANES) for sl in range(cw // LANES)]

    for sl, cols in enumerate(slabs):
        stage_ref[sl, 0:seqlen, :] = u_ref[:, cols]
        stage_ref[sl, seqlen:padded, :] = jnp.zeros((padded - seqlen, LANES), f32)

    def perm(i, c):
        row = pl.multiple_of(i * SUBLANES, SUBLANES)
        for sl, cols in enumerate(slabs):
            perm_ref[pl.ds(row, SUBLANES), cols] = stage_ref[sl, pl.ds(i, SUBLANES, stride=seg), :]
        return c
    lax.fori_loop(0, seg, perm, 0)

    up = perm_ref[...]
    ub = up.astype(bf16)
    y_ref[...] = d_ref[...] * up
    rowid = lax.broadcasted_iota(jnp.int32, (SUBLANES, half), 0)
    zero = jnp.zeros((SUBLANES, half), f32)

    for dirn in (0, 1):
        rev = dirn == 1
        s_ref[...] = jnp.dot(ub, bmat_ref[dirn], preferred_element_type=f32)
        a_re, a_im = a_ref[dirn, :, :half], a_ref[dirn, :, half:]

        def scan(n, carry, store, rev=rev, a_re=a_re, a_im=a_im):
            sr, si = carry
            i = seg - 1 - n if rev else n
            row = pl.multiple_of(i * SUBLANES, SUBLANES)
            nr = a_re * sr - a_im * si + s_ref[pl.ds(row, SUBLANES), :half]
            ni = a_re * si + a_im * sr + s_ref[pl.ds(row, SUBLANES), half:]
            if store:
                s_ref[pl.ds(row, SUBLANES), :half] = nr
                s_ref[pl.ds(row, SUBLANES), half:] = ni
            return nr, ni

        end_r, end_i = lax.fori_loop(0, seg, functools.partial(scan, store=False),
                                     (zero, zero), unroll=2)

        p_re, p_im = aseg_ref[dirn, :, :half], aseg_ref[dirn, :, half:]
        x_re, x_im = zero, zero
        edge = SUBLANES - 1 if rev else 0
        shift = SUBLANES - 1 if rev else 1
        for _ in range(SUBLANES - 1):
            t_re = end_r + p_re * x_re - p_im * x_im
            t_im = end_i + p_re * x_im + p_im * x_re
            x_re = jnp.where(rowid == edge, 0.0, pltpu.roll(t_re, shift, 0))
            x_im = jnp.where(rowid == edge, 0.0, pltpu.roll(t_im, shift, 0))

        lax.fori_loop(0, seg, functools.partial(scan, store=True), (x_re, x_im), unroll=2)

        y_ref[...] += jnp.dot(s_ref[...].astype(bf16), cmat_ref[dirn], preferred_element_type=f32)

    y_ref[...] = jax.nn.gelu(y_ref[...])

    def unperm(i, c):
        row = pl.multiple_of(i * SUBLANES, SUBLANES)
        for sl, cols in enumerate(slabs):
            stage_ref[sl, pl.ds(i, SUBLANES, stride=seg), :] = y_ref[pl.ds(row, SUBLANES), cols]
        return c
    lax.fori_loop(0, seg, unperm, 0)
    for sl, cols in enumerate(slabs):
        o_ref[:, cols] = stage_ref[sl, 0:seqlen, :]


def _s5_tables(lam_re, lam_im, log_step, b_re, b_im, c_re, c_im, seg):
    depth, _, g, p = lam_re.shape
    h = b_re.shape[-1]
    gb = S5_GROUPS_PER_BLOCK
    nb = g // gb
    lr, li = lam_re.astype(f32), lam_im.astype(f32)
    step = jnp.exp(log_step.astype(f32))[..., None]
    mag = jnp.exp(lr * step)
    ang = li * step
    ar, ai = mag * jnp.cos(ang), mag * jnp.sin(ang)
    den = lr * lr + li * li
    fr = ((ar - 1.0) * lr + ai * li) / den
    fi = (ai * lr - (ar - 1.0) * li) / den
    br, bi = b_re.astype(f32), b_im.astype(f32)
    bbar = jnp.stack([fr[..., None] * br - fi[..., None] * bi,
                      fr[..., None] * bi + fi[..., None] * br], axis=-1)
    eye = jnp.eye(gb, dtype=f32)
    bmat = jnp.einsum('ldjgphc,gk->ldjghckp', bbar.reshape(depth, 2, nb, gb, p, h, 2), eye)
    bmat = bmat.reshape(depth, 2, nb, gb * h, 2 * gb * p).astype(bf16)
    cc = jnp.stack([c_re.astype(f32), -c_im.astype(f32)], axis=-1)
    cmat = jnp.einsum('ldjghpc,gk->ldjcgpkh', cc.reshape(depth, 2, nb, gb, h, p, 2), eye)
    cmat = cmat.reshape(depth, 2, nb, 2 * gb * p, gb * h).astype(bf16)

    def rows(re, im):
        v = jnp.concatenate([re.reshape(depth, 2, nb, 1, gb * p), im.reshape(depth, 2, nb, 1, gb * p)],
                            axis=-1)
        return jnp.broadcast_to(v, (depth, 2, nb, SUBLANES, 2 * gb * p))

    a = rows(ar, ai)
    smag = jnp.exp(seg * (lr * step))
    aseg = rows(smag * jnp.cos(seg * ang), smag * jnp.sin(seg * ang))
    return bmat, cmat, a, aseg


def _s5(proj, nseq, seqlen, tables, layer, d, width):
    bmat, cmat, a, aseg = tables
    n = proj.shape[0]
    nb, cw, sw = bmat.shape[2], bmat.shape[3], bmat.shape[4]
    seg = seqlen // SUBLANES + S5_SEG_PAD
    padded = seg * SUBLANES
    return pl.pallas_call(
        functools.partial(_s5_kernel, seg=seg),
        grid=(nseq, nb),
        in_specs=[
            pl.BlockSpec((seqlen, cw), lambda b, j: (b, j)),
            pl.BlockSpec((None, 2, None, cw, sw), lambda b, j: (layer, 0, j, 0, 0)),
            pl.BlockSpec((None, 2, None, sw, cw), lambda b, j: (layer, 0, j, 0, 0)),
            pl.BlockSpec((None, 2, None, SUBLANES, sw), lambda b, j: (layer, 0, j, 0, 0)),
            pl.BlockSpec((None, 2, None, SUBLANES, sw), lambda b, j: (layer, 0, j, 0, 0)),
            pl.BlockSpec((1, cw), lambda b, j: (0, j)),
        ],
        out_specs=pl.BlockSpec((seqlen, cw), lambda b, j: (b, j)),
        out_shape=jax.ShapeDtypeStruct((n, width), f32),
        scratch_shapes=[pltpu.VMEM((cw // LANES, padded, LANES), f32), pltpu.VMEM((padded, cw), f32),
                        pltpu.VMEM((padded, sw), f32), pltpu.VMEM((padded, cw), f32)],
        compiler_params=_params(("parallel", "arbitrary")),
        name="s5_scan",
    )(proj, bmat, cmat, a, aseg, d)


def _glu_kernel(z_ref, w_ref, b_ref, o_ref):
    z = z_ref[...]
    gate = jnp.dot(z.astype(bf16), w_ref[...], preferred_element_type=f32) + b_ref[...]
    o_ref[...] = (z * jax.nn.sigmoid(gate)).astype(o_ref.dtype)


def _glu(z, w, layer, b):
    n, c = z.shape
    tm = _pick(n, (1024, 512, 256, 128))
    return pl.pallas_call(
        _glu_kernel,
        grid=(n // tm,),
        in_specs=[
            pl.BlockSpec((tm, c), lambda i: (i, 0)),
            pl.BlockSpec((None, c, c), lambda i: (layer, 0, 0)),
            pl.BlockSpec((1, c), lambda i: (0, 0)),
        ],
        out_specs=pl.BlockSpec((tm, c), lambda i: (i, 0)),
        out_shape=jax.ShapeDtypeStruct((n, c), bf16),
        compiler_params=_params(("parallel",)),
        name="s5_glu",
    )(z, w, b)


CONV_ROWS = 64
CONV_HALO = 16


def _conv_kernel(v_ref, g_ref, w_ref, b_ref, o_ref, hp_ref, *, taps, pad):
    seqlen, cw = v_ref.shape
    hp_ref[0:CONV_HALO, :] = jnp.zeros((CONV_HALO, cw), f32)
    hp_ref[seqlen + CONV_HALO:seqlen + 2 * CONV_HALO, :] = jnp.zeros((CONV_HALO, cw), f32)
    hp_ref[CONV_HALO:seqlen + CONV_HALO, :] = v_ref[...] * jax.nn.sigmoid(g_ref[...])
    groups = CONV_ROWS // SUBLANES
    bias = jnp.broadcast_to(b_ref[...], (groups, SUBLANES, cw))
    first = CONV_HALO - pad
    span = (first + taps - 1) // SUBLANES * SUBLANES + CONV_ROWS

    def tile(i, c):
        base = pl.multiple_of(i * CONV_ROWS, CONV_ROWS)
        win = hp_ref[pl.ds(base, CONV_ROWS + 2 * CONV_HALO), :]
        acc = bias
        for phase in range(SUBLANES):
            ks = [k for k in range(taps) if (first + k) % SUBLANES == phase]
            if not ks:
                continue
            shifted = win[phase:phase + span, :]
            for k in ks:
                q = (first + k) // SUBLANES * SUBLANES
                acc = acc + w_ref[k] * shifted[q:q + CONV_ROWS, :].reshape(groups, SUBLANES, cw)
        o_ref[pl.ds(base, CONV_ROWS), :] = acc.reshape(CONV_ROWS, cw)
        return c
    lax.fori_loop(0, seqlen // CONV_ROWS, tile, 0)


def _conv(proj, nseq, seqlen, v_col0, g_col0, width, conv_w, layer, conv_b):
    n = proj.shape[0]
    taps = conv_w.shape[1]
    pad = (taps - 1) // 2
    assert pad <= CONV_HALO and CONV_HALO - pad + taps - 1 + CONV_ROWS <= CONV_ROWS + 2 * CONV_HALO
    cw = _pick(width, (256, 128))
    voff, goff = v_col0 // cw, g_col0 // cw
    return pl.pallas_call(
        functools.partial(_conv_kernel, taps=taps, pad=pad),
        grid=(nseq, width // cw),
        in_specs=[
            pl.BlockSpec((seqlen, cw), lambda b, j: (b, j + voff)),
            pl.BlockSpec((seqlen, cw), lambda b, j: (b, j + goff)),
            pl.BlockSpec((None, taps, SUBLANES, cw), lambda b, j: (layer, 0, 0, j)),
            pl.BlockSpec((1, cw), lambda b, j: (0, j)),
        ],
        out_specs=pl.BlockSpec((seqlen, cw), lambda b, j: (b, j)),
        out_shape=jax.ShapeDtypeStruct((n, width), f32),
        scratch_shapes=[pltpu.VMEM((seqlen + 2 * CONV_HALO, cw), f32)],
        compiler_params=_params(("parallel", "arbitrary")),
        name="conv",
    )(proj, proj, conv_w, conv_b)


def _ln_silu_kernel(x_ref, g_ref, b_ref, o_ref):
    y = _layer_norm(x_ref[...], g_ref[...], b_ref[...])
    o_ref[...] = (y * jax.nn.sigmoid(y)).astype(o_ref.dtype)


def _ln_silu(x, g, b):
    n, c = x.shape
    tm = _pick(n, (1024, 512, 256, 128))
    return pl.pallas_call(
        _ln_silu_kernel,
        grid=(n // tm,),
        in_specs=[
            pl.BlockSpec((tm, c), lambda i: (i, 0)),
            pl.BlockSpec((1, c), lambda i: (0, 0)),
            pl.BlockSpec((1, c), lambda i: (0, 0)),
        ],
        out_specs=pl.BlockSpec((tm, c), lambda i: (i, 0)),
        out_shape=jax.ShapeDtypeStruct((n, c), bf16),
        compiler_params=_params(("parallel",)),
        name="conv_ln",
    )(x, g, b)


def _attn_kernel(slopes_ref, lam_ref, q_ref, k_ref, v_ref, g_ref, o_ref, bias_ref, *, scale, post, dh):
    h = pl.program_id(0)
    b = pl.program_id(1)
    i = pl.program_id(2)
    tq = q_ref.shape[0]
    seqlen = k_ref.shape[0]
    lam = lam_ref[0]

    @pl.when(b == 0)
    def _():
        rel = (lax.broadcasted_iota(jnp.int32, (tq, seqlen), 0)
               - lax.broadcasted_iota(jnp.int32, (tq, seqlen), 1))
        bias_ref[i] = (slopes_ref[h] * LOG2E) * jnp.abs(rel + i * tq).astype(f32)

    for r in range(tq // ATTN_ROWS):
        rows = slice(r * ATTN_ROWS, (r + 1) * ATTN_ROWS)
        bias = bias_ref[i, rows, :]
        es, rs = [], []
        for c in (0, 1):
            s = lax.dot_general(q_ref[rows, c * dh:(c + 1) * dh], k_ref[:, c * dh:(c + 1) * dh],
                                (((1,), (1,)), ((), ())), preferred_element_type=f32)
            s = s * (scale * LOG2E) - bias
            e = jnp.exp2(s - jnp.max(s, axis=-1, keepdims=True))
            es.append(e.astype(bf16))
            rs.append(1.0 / jnp.sum(e, axis=-1, keepdims=True))
        ov = jnp.dot(jnp.concatenate(es, axis=0), v_ref[...], preferred_element_type=f32)
        o = ov[:ATTN_ROWS] * rs[0] - ov[ATTN_ROWS:] * (lam * rs[1])
        ms = jnp.mean(o * o, axis=-1, keepdims=True)
        o_ref[rows, :] = (o * lax.rsqrt(ms + LN_EPS) * g_ref[...] * post).astype(o_ref.dtype)


def _attn(qkv, nseq, seqlen, heads, dh, slopes, lam, g, post):
    n = qkv.shape[0]
    hw = 2 * dh
    tq = _pick(seqlen, (2048, 1024, 512, 256, 128))
    nq = seqlen // tq
    return pl.pallas_call(
        functools.partial(_attn_kernel, scale=dh ** -0.5, post=post, dh=dh),
        grid=(heads, nseq, nq),
        in_specs=[
            pl.BlockSpec(memory_space=pltpu.SMEM),
            pl.BlockSpec(memory_space=pltpu.SMEM),
            pl.BlockSpec((tq, hw), lambda h, b, i: (b * nq + i, h)),
            pl.BlockSpec((seqlen, hw), lambda h, b, i: (b, heads + h)),
            pl.BlockSpec((seqlen, hw), lambda h, b, i: (b, 2 * heads + h)),
            pl.BlockSpec((1, hw), lambda h, b, i: (0, 0)),
        ],
        out_specs=pl.BlockSpec((tq, hw), lambda h, b, i: (b * nq + i, h)),
        out_shape=jax.ShapeDtypeStruct((n, heads * hw), bf16),
        scratch_shapes=[pltpu.VMEM((nq, tq, seqlen), f32)],
        compiler_params=_params(("arbitrary", "arbitrary", "arbitrary")),
        name="diff_attn",
    )(slopes, lam, qkv, qkv, qkv, g)


def _out_proj_kernel(ya_ref, yb_ref, yc_ref, w_ref, x_ref, g_ref, b_ref, o_ref, *, alpha):
    ca, cb = ya_ref.shape[1], yb_ref.shape[1]
    mix = jnp.dot(ya_ref[...], w_ref[0:ca, :], preferred_element_type=f32)
    mix += jnp.dot(yb_ref[...], w_ref[ca:ca + cb, :], preferred_element_type=f32)
    mix += jnp.dot(yc_ref[...], w_ref[ca + cb:, :], preferred_element_type=f32)
    o_ref[...] = _layer_norm(alpha * x_ref[...] + mix, g_ref[...], b_ref[...])


def _out_proj(ya, yb, yc, w, layer, x, g, b, alpha):
    n, d = x.shape
    tm = _pick(n, (256, 128))
    row_spec = lambda a: pl.BlockSpec((tm, a.shape[1]), lambda i: (i, 0))
    return pl.pallas_call(
        functools.partial(_out_proj_kernel, alpha=alpha),
        grid=(n // tm,),
        in_specs=[
            row_spec(ya), row_spec(yb), row_spec(yc),
            pl.BlockSpec((None,) + w.shape[1:], lambda i: (layer, 0, 0), pipeline_mode=pl.Buffered(1)),
            row_spec(x),
            pl.BlockSpec((1, d), lambda i: (0, 0)),
            pl.BlockSpec((1, d), lambda i: (0, 0)),
        ],
        out_specs=pl.BlockSpec((tm, d), lambda i: (i, 0)),
        out_shape=jax.ShapeDtypeStruct((n, d), f32),
        compiler_params=_params(("parallel",)),
        name="out_proj",
    )(ya, yb, yc, w, x, g, b)


def kernel(x_prompt, x_sample, ffn1_w_in, ffn1_w_out, ln_ffn1_g, ln_ffn1_b, w_in, s5_lambda_re, s5_lambda_im, s5_log_step, s5_b_re, s5_b_im, s5_c_re, s5_c_im, s5_d, s5_glu_w, s5_glu_b, conv_w, conv_b, conv_ln_g, conv_ln_b, attn_lambda_q1, attn_lambda_k1, attn_lambda_q2, attn_lambda_k2, attn_subln_g, w_out, ln_mix_g, ln_mix_b, ffn2_w_in, ffn2_w_out, ln_ffn2_g, ln_ffn2_b):
    depth = w_in.shape[0]
    bp, seqlen, d_model = x_prompt.shape
    bs, seqlen_s, _ = x_sample.shape
    assert seqlen == seqlen_s and seqlen % (SUBLANES * SUBLANES) == 0
    nseq = bp + bs
    n_p, n_s = bp * seqlen, bs * seqlen
    n = n_p + n_s
    alpha = (2 * depth) ** 0.25

    s5_w = s5_d.shape[1]
    conv_wd = conv_b.shape[1]
    dh = attn_lambda_q1.shape[1]
    attn_w = d_model - s5_w - conv_wd
    heads = attn_w // (2 * dh)
    f32_cols = s5_w + 2 * conv_wd
    seg = seqlen // SUBLANES + S5_SEG_PAD

    x = jnp.concatenate([x_prompt.reshape(n_p, d_model), x_sample.reshape(n_s, d_model)],
                        axis=0).astype(f32)
    slopes = jnp.asarray([2.0 ** (-8.0 * (h + 1) / heads) for h in range(heads)], dtype=f32)
    row = lambda v: v.astype(f32).reshape(1, -1)

    ffn1_wgu, ffn1_wo = _ffn_weights(ffn1_w_in, ffn1_w_out)
    ffn2_wgu, ffn2_wo = _ffn_weights(ffn2_w_in, ffn2_w_out)
    w_in_b, w_out_b, glu_w_b = w_in.astype(bf16), w_out.astype(bf16), s5_glu_w.astype(bf16)
    conv_w_f = jnp.broadcast_to(conv_w.astype(f32)[:, :, None, :],
                                conv_w.shape[:2] + (SUBLANES, conv_wd))
    tables = _s5_tables(s5_lambda_re, s5_lambda_im, s5_log_step, s5_b_re, s5_b_im, s5_c_re,
                        s5_c_im, seg)

    for l in range(depth):
        x, xb = _ffn(x, 0, n, ffn1_wgu, ffn1_wo, l, row(ln_ffn1_g[l]), row(ln_ffn1_b[l]), alpha, True)

        proj = _matmul(xb, w_in_b, l, 0, f32_cols, f32)
        qkv = _matmul(xb, w_in_b, l, f32_cols, 3 * attn_w, bf16)

        z = _s5(proj, nseq, seqlen, tables, l, row(s5_d[l]), s5_w)
        y_a = _glu(z, glu_w_b, l, row(s5_glu_b[l]))

        c = _conv(proj, nseq, seqlen, s5_w, s5_w + conv_wd, conv_wd, conv_w_f, l, row(conv_b[l]))
        y_b = _ln_silu(c, row(conv_ln_g[l]), row(conv_ln_b[l]))

        lam_init = 0.8 - 0.6 * math.exp(-0.3 * l)
        lam = (jnp.exp(jnp.sum(attn_lambda_q1[l].astype(f32) * attn_lambda_k1[l].astype(f32)))
               - jnp.exp(jnp.sum(attn_lambda_q2[l].astype(f32) * attn_lambda_k2[l].astype(f32)))
               + lam_init).reshape(1)
        y_c = _attn(qkv, nseq, seqlen, heads, dh, slopes, lam, row(attn_subln_g[l]), 1.0 - lam_init)

        x = _out_proj(y_a, y_b, y_c, w_out_b, l, x, row(ln_mix_g[l]), row(ln_mix_b[l]), alpha)

        g2, b2 = row(ln_ffn2_g[l]), row(ln_ffn2_b[l])
        if l + 1 < depth:
            x, = _ffn(x, 0, n, ffn2_wgu, ffn2_wo, l, g2, b2, alpha, False)
        else:
            y_p, = _ffn(x, 0, n_p, ffn2_wgu, ffn2_wo, l, g2, b2, alpha, False)
            y_s, = _ffn(x, n_p, n_s, ffn2_wgu, ffn2_wo, l, g2, b2, alpha, False)

    y_prompt = y_p.reshape(bp, seqlen, d_model).astype(x_prompt.dtype)
    y_sample = y_s.reshape(bs, seqlen, d_model).astype(x_sample.dtype)
    return (y_prompt, y_sample)
```

```python
import functools
import math

import jax
import jax.numpy as jnp
from jax import lax
from jax.experimental import pallas as pl
from jax.experimental.pallas import tpu as pltpu

LN_EPS = 1e-5
S5_GROUPS_PER_BLOCK = 16
SUBLANES = 8
LANES = 128
S5_SEG_PAD = 4
VMEM_LIMIT = 60 * 1024 * 1024
FFN_TF = 256
ATTN_ROWS = 256
LOG2E = 1.4426950408889634
LN_ROWS = 16
LN_GROUP = 8

f32 = jnp.float32
bf16 = jnp.bfloat16


def _pick(n, candidates):
    for c in candidates:
        if n % c == 0:
            return c
    raise ValueError(f"no tile for {n} in {candidates}")


def _params(sem):
    return pltpu.CompilerParams(dimension_semantics=sem, vmem_limit_bytes=VMEM_LIMIT)


def _layer_norm(y, g, b, eps=LN_EPS):
    mu = jnp.mean(y, axis=-1, keepdims=True)
    yc = y - mu
    var = jnp.mean(yc * yc, axis=-1, keepdims=True)
    return yc * lax.rsqrt(var + eps) * g + b


def _layer_norm_rows(z_ref, x_ref, g_ref, b_ref, o_ref, obf_ref, *, scale, eps):
    g, b = g_ref[...], b_ref[...]
    span = LN_ROWS * LN_GROUP

    def group(r, c):
        base = pl.multiple_of(r * span, span)
        rows = [pl.ds(base + j * LN_ROWS, LN_ROWS) for j in range(LN_GROUP)]
        zs = [z_ref[rw, :] for rw in rows]
        if x_ref is not None:
            zs = [z + scale * x_ref[rw, :] for z, rw in zip(zs, rows)]
        ys = [_layer_norm(z, g, b, eps) for z in zs]
        for y, rw in zip(ys, rows):
            o_ref[rw, :] = y
            if obf_ref is not None:
                obf_ref[rw, :] = y.astype(bf16)
        return c
    lax.fori_loop(0, z_ref.shape[0] // span, group, 0)


def _ffn_kernel(x_ref, wgu_ref, wo_ref, g_ref, b_ref, o_ref, *rest, alpha, tf):
    obf_ref = rest[0] if len(rest) == 2 else None
    xbf_ref = rest[-1]
    k = pl.program_id(1)

    @pl.when(k == 0)
    def _():
        x = x_ref[...]
        xbf_ref[...] = x.astype(bf16)
        o_ref[...] = (2.0 * alpha) * x

    gu = jnp.dot(xbf_ref[...], wgu_ref[...], preferred_element_type=f32)
    gate, up = gu[:, :tf], gu[:, tf:]
    h = (gate * jax.nn.sigmoid(gate) * up).astype(bf16)
    o_ref[...] += jnp.dot(h, wo_ref[...], preferred_element_type=f32)

    @pl.when(k == pl.num_programs(1) - 1)
    def _():
        _layer_norm_rows(o_ref, None, g_ref, b_ref, o_ref, obf_ref, scale=1.0, eps=4.0 * LN_EPS)


def _regroup_kernel(g_ref, u_ref, o_ref):
    tf = g_ref.shape[1]
    o_ref[:, :tf] = g_ref[...].astype(bf16)
    o_ref[:, tf:] = u_ref[...].astype(bf16)


def _ffn_weights(w_in, w_out):
    depth, d, two_dff = w_in.shape
    dff = two_dff // 2
    tf = _pick(dff, (FFN_TF, 128))
    nk = dff // tf
    wgu = pl.pallas_call(
        _regroup_kernel,
        grid=(depth, nk),
        in_specs=[
            pl.BlockSpec((None, d, tf), lambda l, k: (l, 0, k)),
            pl.BlockSpec((None, d, tf), lambda l, k: (l, 0, k + nk)),
        ],
        out_specs=pl.BlockSpec((None, None, d, 2 * tf), lambda l, k: (l, k, 0, 0)),
        out_shape=jax.ShapeDtypeStruct((depth, nk, d, 2 * tf), bf16),
        compiler_params=_params(("parallel", "parallel")),
        name="ffn_weight_regroup",
    )(w_in, w_in)
    return wgu, w_out.astype(bf16)


def _ffn(x, row0, nrows, wgu, wo, layer, g, b, alpha, want_bf16):
    d = x.shape[1]
    nk, tf = wgu.shape[1], wgu.shape[3] // 2
    tm = _pick(math.gcd(nrows, row0) if row0 else nrows, (512, 256, 128))
    roff = row0 // tm
    out_specs = [pl.BlockSpec((tm, d), lambda i, k: (i, 0))]
    out_shape = [jax.ShapeDtypeStruct((nrows, d), f32)]
    if want_bf16:
        out_specs.append(pl.BlockSpec((tm, d), lambda i, k: (i, 0)))
        out_shape.append(jax.ShapeDtypeStruct((nrows, d), bf16))
    return pl.pallas_call(
        functools.partial(_ffn_kernel, alpha=alpha, tf=tf),
        grid=(nrows // tm, nk),
        in_specs=[
            pl.BlockSpec((tm, d), lambda i, k: (i + roff, 0)),
            pl.BlockSpec((None, None, d, 2 * tf), lambda i, k: (layer, k, 0, 0)),
            pl.BlockSpec((None, tf, d), lambda i, k: (layer, k, 0)),
            pl.BlockSpec((1, d), lambda i, k: (0, 0)),
            pl.BlockSpec((1, d), lambda i, k: (0, 0)),
        ],
        out_specs=out_specs,
        out_shape=out_shape,
        scratch_shapes=[pltpu.VMEM((tm, d), bf16)],
        compiler_params=_params(("parallel", "arbitrary")),
        name="ffn",
    )(x, wgu, wo, g, b)


def _matmul_kernel(x_ref, w_ref, o_ref):
    o_ref[...] = jnp.dot(x_ref[...], w_ref[...], preferred_element_type=f32).astype(o_ref.dtype)


def _matmul(x, w, layer, col0, ncols, out_dtype):
    n, d = x.shape
    tm = _pick(n, (1024, 512, 256, 128))
    tn = _pick(math.gcd(ncols, col0) if col0 else ncols, (1024, 768, 512, 256, 128))
    off = col0 // tn
    return pl.pallas_call(
        _matmul_kernel,
        grid=(n // tm, ncols // tn),
        in_specs=[
            pl.BlockSpec((tm, d), lambda i, j: (i, 0)),
            pl.BlockSpec((None, d, tn), lambda i, j: (layer, 0, j + off)),
        ],
        out_specs=pl.BlockSpec((tm, tn), lambda i, j: (i, j)),
        out_shape=jax.ShapeDtypeStruct((n, ncols), out_dtype),
        compiler_params=_params(("parallel", "arbitrary")),
        name="in_proj",
    )(x, w)


def _s5_kernel(u_ref, bmat_ref, cmat_ref, a_ref, aseg_ref, d_ref, o_ref,
               stage_ref, perm_ref, s_ref, y_ref, *, seg):
    seqlen, cw = u_ref.shape
    padded = seg * SUBLANES
    half = s_ref.shape[1] // 2
    slabs = [slice(sl * LANES, (sl + 1) * LANES) for sl in range(cw // LANES)]

    for sl, cols in enumerate(slabs):
        stage_ref[sl, 0:seqlen, :] = u_ref[:, cols]
        stage_ref[sl, seqlen:padded, :] = jnp.zeros((padded - seqlen, LANES), f32)

    def perm(i, c):
        row = pl.multiple_of(i * SUBLANES, SUBLANES)
        for sl, cols in enumerate(slabs):
            perm_ref[pl.ds(row, SUBLANES), cols] = stage_ref[sl, pl.ds(i, SUBLANES, stride=seg), :]
        return c
    lax.fori_loop(0, seg, perm, 0)

    up = perm_ref[...]
    ub = up.astype(bf16)
    y_ref[...] = d_ref[...] * up
    rowid = lax.broadcasted_iota(jnp.int32, (SUBLANES, half), 0)
    zero = jnp.zeros((SUBLANES, half), f32)

    for dirn in (0, 1):
        rev = dirn == 1
        s_ref[...] = jnp.dot(ub, bmat_ref[dirn], preferred_element_type=f32)
        a_re, a_im = a_ref[dirn, :, :half], a_ref[dirn, :, half:]

        def scan(n, carry, store, rev=rev, a_re=a_re, a_im=a_im):
            sr, si = carry
            i = seg - 1 - n if rev else n
            row = pl.multiple_of(i * SUBLANES, SUBLANES)
            nr = a_re * sr - a_im * si + s_ref[pl.ds(row, SUBLANES), :half]
            ni = a_re * si + a_im * sr + s_ref[pl.ds(row, SUBLANES), half:]
            if store:
                s_ref[pl.ds(row, SUBLANES), :half] = nr
                s_ref[pl.ds(row, SUBLANES), half:] = ni
            return nr, ni

        end_r, end_i = lax.fori_loop(0, seg, functools.partial(scan, store=False),
                                     (zero, zero), unroll=2)

        p_re, p_im = aseg_ref[dirn, :, :half], aseg_ref[dirn, :, half:]
        x_re, x_im = zero, zero
        edge = SUBLANES - 1 if rev else 0
        shift = SUBLANES - 1 if rev else 1
        for _ in range(SUBLANES - 1):
            t_re = end_r + p_re * x_re - p_im * x_im
            t_im = end_i + p_re * x_im + p_im * x_re
            x_re = jnp.where(rowid == edge, 0.0, pltpu.roll(t_re, shift, 0))
            x_im = jnp.where(rowid == edge, 0.0, pltpu.roll(t_im, shift, 0))

        lax.fori_loop(0, seg, functools.partial(scan, store=True), (x_re, x_im), unroll=2)

        y_ref[...] += jnp.dot(s_ref[...].astype(bf16), cmat_ref[dirn], preferred_element_type=f32)

    y_ref[...] = jax.nn.gelu(y_ref[...])

    def unperm(i, c):
        row = pl.multiple_of(i * SUBLANES, SUBLANES)
        for sl, cols in enumerate(slabs):
            stage_ref[sl, pl.ds(i, SUBLANES, stride=seg), :] = y_ref[pl.ds(row, SUBLANES), cols]
        return c
    lax.fori_loop(0, seg, unperm, 0)
    for sl, cols in enumerate(slabs):
        o_ref[:, cols] = stage_ref[sl, 0:seqlen, :]


def _s5_tables(lam_re, lam_im, log_step, b_re, b_im, c_re, c_im, seg):
    depth, _, g, p = lam_re.shape
    h = b_re.shape[-1]
    gb = S5_GROUPS_PER_BLOCK
    nb = g // gb
    lr, li = lam_re.astype(f32), lam_im.astype(f32)
    step = jnp.exp(log_step.astype(f32))[..., None]
    mag = jnp.exp(lr * step)
    ang = li * step
    ar, ai = mag * jnp.cos(ang), mag * jnp.sin(ang)
    den = lr * lr + li * li
    fr = ((ar - 1.0) * lr + ai * li) / den
    fi = (ai * lr - (ar - 1.0) * li) / den
    br, bi = b_re.astype(f32), b_im.astype(f32)
    bbar = jnp.stack([fr[..., None] * br - fi[..., None] * bi,
                      fr[..., None] * bi + fi[..., None] * br], axis=-1)
    eye = jnp.eye(gb, dtype=f32)
    bmat = jnp.einsum('ldjgphc,gk->ldjghckp', bbar.reshape(depth, 2, nb, gb, p, h, 2), eye)
    bmat = bmat.reshape(depth, 2, nb, gb * h, 2 * gb * p).astype(bf16)
    cc = jnp.stack([c_re.astype(f32), -c_im.astype(f32)], axis=-1)
    cmat = jnp.einsum('ldjghpc,gk->ldjcgpkh', cc.reshape(depth, 2, nb, gb, h, p, 2), eye)
    cmat = cmat.reshape(depth, 2, nb, 2 * gb * p, gb * h).astype(bf16)

    def rows(re, im):
        v = jnp.concatenate([re.reshape(depth, 2, nb, 1, gb * p), im.reshape(depth, 2, nb, 1, gb * p)],
                            axis=-1)
        return jnp.broadcast_to(v, (depth, 2, nb, SUBLANES, 2 * gb * p))

    a = rows(ar, ai)
    smag = jnp.exp(seg * (lr * step))
    aseg = rows(smag * jnp.cos(seg * ang), smag * jnp.sin(seg * ang))
    return bmat, cmat, a, aseg


def _s5(proj, nseq, seqlen, tables, layer, d, width):
    bmat, cmat, a, aseg = tables
    n = proj.shape[0]
    nb, cw, sw = bmat.shape[2], bmat.shape[3], bmat.shape[4]
    seg = seqlen // SUBLANES + S5_SEG_PAD
    padded = seg * SUBLANES
    return pl.pallas_call(
        functools.partial(_s5_kernel, seg=seg),
        grid=(nseq, nb),
        in_specs=[
            pl.BlockSpec((seqlen, cw), lambda b, j: (b, j)),
            pl.BlockSpec((None, 2, None, cw, sw), lambda b, j: (layer, 0, j, 0, 0)),
            pl.BlockSpec((None, 2, None, sw, cw), lambda b, j: (layer, 0, j, 0, 0)),
            pl.BlockSpec((None, 2, None, SUBLANES, sw), lambda b, j: (layer, 0, j, 0, 0)),
            pl.BlockSpec((None, 2, None, SUBLANES, sw), lambda b, j: (layer, 0, j, 0, 0)),
            pl.BlockSpec((1, cw), lambda b, j: (0, j)),
        ],
        out_specs=pl.BlockSpec((seqlen, cw), lambda b, j: (b, j)),
        out_shape=jax.ShapeDtypeStruct((n, width), f32),
        scratch_shapes=[pltpu.VMEM((cw // LANES, padded, LANES), f32), pltpu.VMEM((padded, cw), f32),
                        pltpu.VMEM((padded, sw), f32), pltpu.VMEM((padded, cw), f32)],
        compiler_params=_params(("parallel", "arbitrary")),
        name="s5_scan",
    )(proj, bmat, cmat, a, aseg, d)


def _glu_kernel(z_ref, w_ref, b_ref, o_ref):
    z = z_ref[...]
    gate = jnp.dot(z.astype(bf16), w_ref[...], preferred_element_type=f32) + b_ref[...]
    o_ref[...] = (z * jax.nn.sigmoid(gate)).astype(o_ref.dtype)


def _glu(z, w, layer, b):
    n, c = z.shape
    tm = _pick(n, (1024, 512, 256, 128))
    return pl.pallas_call(
        _glu_kernel,
        grid=(n // tm,),
        in_specs=[
            pl.BlockSpec((tm, c), lambda i: (i, 0)),
            pl.BlockSpec((None, c, c), lambda i: (layer, 0, 0)),
            pl.BlockSpec((1, c), lambda i: (0, 0)),
        ],
        out_specs=pl.BlockSpec((tm, c), lambda i: (i, 0)),
        out_shape=jax.ShapeDtypeStruct((n, c), bf16),
        compiler_params=_params(("parallel",)),
        name="s5_glu",
    )(z, w, b)


CONV_ROWS = 64
CONV_HALO = 16


def _conv_kernel(v_ref, g_ref, w_ref, b_ref, o_ref, hp_ref, *, taps, pad):
    seqlen, cw = v_ref.shape
    hp_ref[0:CONV_HALO, :] = jnp.zeros((CONV_HALO, cw), f32)
    hp_ref[seqlen + CONV_HALO:seqlen + 2 * CONV_HALO, :] = jnp.zeros((CONV_HALO, cw), f32)
    hp_ref[CONV_HALO:seqlen + CONV_HALO, :] = v_ref[...] * jax.nn.sigmoid(g_ref[...])
    groups = CONV_ROWS // SUBLANES
    bias = jnp.broadcast_to(b_ref[...], (groups, SUBLANES, cw))
    first = CONV_HALO - pad
    span = (first + taps - 1) // SUBLANES * SUBLANES + CONV_ROWS

    def tile(i, c):
        base = pl.multiple_of(i * CONV_ROWS, CONV_ROWS)
        win = hp_ref[pl.ds(base, CONV_ROWS + 2 * CONV_HALO), :]
        acc = bias
        for phase in range(SUBLANES):
            ks = [k for k in range(taps) if (first + k) % SUBLANES == phase]
            if not ks:
                continue
            shifted = win[phase:phase + span, :]
            for k in ks:
                q = (first + k) // SUBLANES * SUBLANES
                acc = acc + w_ref[k] * shifted[q:q + CONV_ROWS, :].reshape(groups, SUBLANES, cw)
        o_ref[pl.ds(base, CONV_ROWS), :] = acc.reshape(CONV_ROWS, cw)
        return c
    lax.fori_loop(0, seqlen // CONV_ROWS, tile, 0)


def _conv(proj, nseq, seqlen, v_col0, g_col0, width, conv_w, layer, conv_b):
    n = proj.shape[0]
    taps = conv_w.shape[1]
    pad = (taps - 1) // 2
    assert pad <= CONV_HALO and CONV_HALO - pad + taps - 1 + CONV_ROWS <= CONV_ROWS + 2 * CONV_HALO
    cw = _pick(width, (256, 128))
    voff, goff = v_col0 // cw, g_col0 // cw
    return pl.pallas_call(
        functools.partial(_conv_kernel, taps=taps, pad=pad),
        grid=(nseq, width // cw),
        in_specs=[
            pl.BlockSpec((seqlen, cw), lambda b, j: (b, j + voff)),
            pl.BlockSpec((seqlen, cw), lambda b, j: (b, j + goff)),
            pl.BlockSpec((None, taps, SUBLANES, cw), lambda b, j: (layer, 0, 0, j)),
            pl.BlockSpec((1, cw), lambda b, j: (0, j)),
        ],
        out_specs=pl.BlockSpec((seqlen, cw), lambda b, j: (b, j)),
        out_shape=jax.ShapeDtypeStruct((n, width), f32),
        scratch_shapes=[pltpu.VMEM((seqlen + 2 * CONV_HALO, cw), f32)],
        compiler_params=_params(("parallel", "arbitrary")),
        name="conv",
    )(proj, proj, conv_w, conv_b)


def _ln_silu_kernel(x_ref, g_ref, b_ref, o_ref):
    y = _layer_norm(x_ref[...], g_ref[...], b_ref[...])
    o_ref[...] = (y * jax.nn.sigmoid(y)).astype(o_ref.dtype)


def _ln_silu(x, g, b):
    n, c = x.shape
    tm = _pick(n, (1024, 512, 256, 128))
    return pl.pallas_call(
        _ln_silu_kernel,
        grid=(n // tm,),
        in_specs=[
            pl.BlockSpec((tm, c), lambda i: (i, 0)),
            pl.BlockSpec((1, c), lambda i: (0, 0)),
            pl.BlockSpec((1, c), lambda i: (0, 0)),
        ],
        out_specs=pl.BlockSpec((tm, c), lambda i: (i, 0)),
        out_shape=jax.ShapeDtypeStruct((n, c), bf16),
        compiler_params=_params(("parallel",)),
        name="conv_ln",
    )(x, g, b)


def _attn_kernel(slopes_ref, lam_ref, q_ref, k_ref, v_ref, g_ref, o_ref, bias_ref, *, scale, post, dh):
    h = pl.program_id(0)
    b = pl.program_id(1)
    i = pl.program_id(2)
    tq = q_ref.shape[0]
    seqlen = k_ref.shape[0]
    lam = lam_ref[0]

    @pl.when(b == 0)
    def _():
        rel = (lax.broadcasted_iota(jnp.int32, (tq, seqlen), 0)
               - lax.broadcasted_iota(jnp.int32, (tq, seqlen), 1))
        bias_ref[i] = (slopes_ref[h] * LOG2E) * jnp.abs(rel + i * tq).astype(f32)

    for r in range(tq // ATTN_ROWS):
        rows = slice(r * ATTN_ROWS, (r + 1) * ATTN_ROWS)
        bias = bias_ref[i, rows, :]
        es, rs = [], []
        for c in (0, 1):
            s = lax.dot_general(q_ref[rows, c * dh:(c + 1) * dh], k_ref[:, c * dh:(c + 1) * dh],
                                (((1,), (1,)), ((), ())), preferred_element_type=f32)
            s = s * (scale * LOG2E) - bias
            e = jnp.exp2(s - jnp.max(s, axis=-1, keepdims=True))
            es.append(e.astype(bf16))
            rs.append(1.0 / jnp.sum(e, axis=-1, keepdims=True))
        ov = jnp.dot(jnp.concatenate(es, axis=0), v_ref[...], preferred_element_type=f32)
        o = ov[:ATTN_ROWS] * rs[0] - ov[ATTN_ROWS:] * (lam * rs[1])
        ms = jnp.mean(o * o, axis=-1, keepdims=True)
        o_ref[rows, :] = (o * lax.rsqrt(ms + LN_EPS) * g_ref[...] * post).astype(o_ref.dtype)


def _attn(qkv, nseq, seqlen, heads, dh, slopes, lam, g, post):
    n = qkv.shape[0]
    hw = 2 * dh
    tq = _pick(seqlen, (2048, 1024, 512, 256, 128))
    nq = seqlen // tq
    return pl.pallas_call(
        functools.partial(_attn_kernel, scale=dh ** -0.5, post=post, dh=dh),
        grid=(heads, nseq, nq),
        in_specs=[
            pl.BlockSpec(memory_space=pltpu.SMEM),
            pl.BlockSpec(memory_space=pltpu.SMEM),
            pl.BlockSpec((tq, hw), lambda h, b, i: (b * nq + i, h)),
            pl.BlockSpec((seqlen, hw), lambda h, b, i: (b, heads + h)),
            pl.BlockSpec((seqlen, hw), lambda h, b, i: (b, 2 * heads + h)),
            pl.BlockSpec((1, hw), lambda h, b, i: (0, 0)),
        ],
        out_specs=pl.BlockSpec((tq, hw), lambda h, b, i: (b * nq + i, h)),
        out_shape=jax.ShapeDtypeStruct((n, heads * hw), bf16),
        scratch_shapes=[pltpu.VMEM((nq, tq, seqlen), f32)],
        compiler_params=_params(("arbitrary", "arbitrary", "arbitrary")),
        name="diff_attn",
    )(slopes, lam, qkv, qkv, qkv, g)


def _out_proj_kernel(ya_ref, yb_ref, yc_ref, w_ref, x_ref, g_ref, b_ref, o_ref, *, alpha):
    ca, cb = ya_ref.shape[1], yb_ref.shape[1]
    mix = jnp.dot(ya_ref[...], w_ref[0:ca, :], preferred_element_type=f32)
    mix += jnp.dot(yb_ref[...], w_ref[ca:ca + cb, :], preferred_element_type=f32)
    mix += jnp.dot(yc_ref[...], w_ref[ca + cb:, :], preferred_element_type=f32)
    o_ref[...] = mix
    _layer_norm_rows(o_ref, x_ref, g_ref, b_ref, o_ref, None, scale=alpha, eps=LN_EPS)


def _out_proj(ya, yb, yc, w, layer, x, g, b, alpha):
    n, d = x.shape
    tm = _pick(n, (256, 128))
    row_spec = lambda a: pl.BlockSpec((tm, a.shape[1]), lambda i: (i, 0))
    return pl.pallas_call(
        functools.partial(_out_proj_kernel, alpha=alpha),
        grid=(n // tm,),
        in_specs=[
            row_spec(ya), row_spec(yb), row_spec(yc),
            pl.BlockSpec((None,) + w.shape[1:], lambda i: (layer, 0, 0), pipeline_mode=pl.Buffered(1)),
            row_spec(x),
            pl.BlockSpec((1, d), lambda i: (0, 0)),
            pl.BlockSpec((1, d), lambda i: (0, 0)),
        ],
        out_specs=pl.BlockSpec((tm, d), lambda i: (i, 0)),
        out_shape=jax.ShapeDtypeStruct((n, d), f32),
        compiler_params=_params(("parallel",)),
        name="out_proj",
    )(ya, yb, yc, w, x, g, b)


def kernel(x_prompt, x_sample, ffn1_w_in, ffn1_w_out, ln_ffn1_g, ln_ffn1_b, w_in, s5_lambda_re, s5_lambda_im, s5_log_step, s5_b_re, s5_b_im, s5_c_re, s5_c_im, s5_d, s5_glu_w, s5_glu_b, conv_w, conv_b, conv_ln_g, conv_ln_b, attn_lambda_q1, attn_lambda_k1, attn_lambda_q2, attn_lambda_k2, attn_subln_g, w_out, ln_mix_g, ln_mix_b, ffn2_w_in, ffn2_w_out, ln_ffn2_g, ln_ffn2_b):
    depth = w_in.shape[0]
    bp, seqlen, d_model = x_prompt.shape
    bs, seqlen_s, _ = x_sample.shape
    assert seqlen == seqlen_s and seqlen % (SUBLANES * SUBLANES) == 0
    nseq = bp + bs
    n_p, n_s = bp * seqlen, bs * seqlen
    n = n_p + n_s
    alpha = (2 * depth) ** 0.25

    s5_w = s5_d.shape[1]
    conv_wd = conv_b.shape[1]
    dh = attn_lambda_q1.shape[1]
    attn_w = d_model - s5_w - conv_wd
    heads = attn_w // (2 * dh)
    f32_cols = s5_w + 2 * conv_wd
    seg = seqlen // SUBLANES + S5_SEG_PAD

    x = jnp.concatenate([x_prompt.reshape(n_p, d_model), x_sample.reshape(n_s, d_model)],
                        axis=0).astype(f32)
    slopes = jnp.asarray([2.0 ** (-8.0 * (h + 1) / heads) for h in range(heads)], dtype=f32)
    row = lambda v: v.astype(f32).reshape(1, -1)

    ffn1_wgu, ffn1_wo = _ffn_weights(ffn1_w_in, ffn1_w_out)
    ffn2_wgu, ffn2_wo = _ffn_weights(ffn2_w_in, ffn2_w_out)
    w_in_b, w_out_b, glu_w_b = w_in.astype(bf16), w_out.astype(bf16), s5_glu_w.astype(bf16)
    conv_w_f = jnp.broadcast_to(conv_w.astype(f32)[:, :, None, :],
                                conv_w.shape[:2] + (SUBLANES, conv_wd))
    tables = _s5_tables(s5_lambda_re, s5_lambda_im, s5_log_step, s5_b_re, s5_b_im, s5_c_re,
                        s5_c_im, seg)

    for l in range(depth):
        x, xb = _ffn(x, 0, n, ffn1_wgu, ffn1_wo, l, row(ln_ffn1_g[l]), row(ln_ffn1_b[l]), alpha, True)

        proj = _matmul(xb, w_in_b, l, 0, f32_cols, f32)
        qkv = _matmul(xb, w_in_b, l, f32_cols, 3 * attn_w, bf16)

        z = _s5(proj, nseq, seqlen, tables, l, row(s5_d[l]), s5_w)
        y_a = _glu(z, glu_w_b, l, row(s5_glu_b[l]))

        c = _conv(proj, nseq, seqlen, s5_w, s5_w + conv_wd, conv_wd, conv_w_f, l, row(conv_b[l]))
        y_b = _ln_silu(c, row(conv_ln_g[l]), row(conv_ln_b[l]))

        lam_init = 0.8 - 0.6 * math.exp(-0.3 * l)
        lam = (jnp.exp(jnp.sum(attn_lambda_q1[l].astype(f32) * attn_lambda_k1[l].astype(f32)))
               - jnp.exp(jnp.sum(attn_lambda_q2[l].astype(f32) * attn_lambda_k2[l].astype(f32)))
               + lam_init).reshape(1)
        y_c = _attn(qkv, nseq, seqlen, heads, dh, slopes, lam, row(attn_subln_g[l]), 1.0 - lam_init)

        x = _out_proj(y_a, y_b, y_c, w_out_b, l, x, row(ln_mix_g[l]), row(ln_mix_b[l]), alpha)

        g2, b2 = row(ln_ffn2_g[l]), row(ln_ffn2_b[l])
        if l + 1 < depth:
            x, = _ffn(x, 0, n, ffn2_wgu, ffn2_wo, l, g2, b2, alpha, False)
        else:
            y_p, = _ffn(x, 0, n_p, ffn2_wgu, ffn2_wo, l, g2, b2, alpha, False)
            y_s, = _ffn(x, n_p, n_s, ffn2_wgu, ffn2_wo, l, g2, b2, alpha, False)

    y_prompt = y_p.reshape(bp, seqlen, d_model).astype(x_prompt.dtype)
    y_sample = y_s.reshape(bs, seqlen, d_model).astype(x_sample.dtype)
    return (y_prompt, y_sample)
```

```python
import functools
import math

import jax
import jax.numpy as jnp
from jax import lax
from jax.experimental import pallas as pl
from jax.experimental.pallas import tpu as pltpu

LN_EPS = 1e-5
S5_GROUPS_PER_BLOCK = 16
SUBLANES = 8
LANES = 128
S5_SEG_PAD = 4
VMEM_LIMIT = 60 * 1024 * 1024
FFN_TF = 256
ATTN_ROWS = 256
LOG2E = 1.4426950408889634
LN_ROWS = 16
LN_GROUP = 8

f32 = jnp.float32
bf16 = jnp.bfloat16


def _pick(n, candidates):
    for c in candidates:
        if n % c == 0:
            return c
    raise ValueError(f"no tile for {n} in {candidates}")


def _params(sem):
    return pltpu.CompilerParams(dimension_semantics=sem, vmem_limit_bytes=VMEM_LIMIT)


def _layer_norm(y, g, b, eps=LN_EPS):
    mu = jnp.mean(y, axis=-1, keepdims=True)
    yc = y - mu
    var = jnp.mean(yc * yc, axis=-1, keepdims=True)
    return yc * lax.rsqrt(var + eps) * g + b


def _layer_norm_rows(z_ref, x_ref, g_ref, b_ref, o_ref, obf_ref, *, scale, eps):
    g, b = g_ref[...], b_ref[...]
    span = LN_ROWS * LN_GROUP

    def group(r, c):
        base = pl.multiple_of(r * span, span)
        rows = [pl.ds(base + j * LN_ROWS, LN_ROWS) for j in range(LN_GROUP)]
        zs = [z_ref[rw, :] for rw in rows]
        if x_ref is not None:
            zs = [z + scale * x_ref[rw, :] for z, rw in zip(zs, rows)]
        ys = [_layer_norm(z, g, b, eps) for z in zs]
        for y, rw in zip(ys, rows):
            o_ref[rw, :] = y
            if obf_ref is not None:
                obf_ref[rw, :] = y.astype(bf16)
        return c
    lax.fori_loop(0, z_ref.shape[0] // span, group, 0)


def _ffn_kernel(x_ref, wgu_ref, wo_ref, g_ref, b_ref, *rest, alpha, tf, n_prev, want_bf16):
    rest = rest[n_prev:]
    o_ref = rest[0]
    obf_ref = rest[1] if want_bf16 else None
    xbf_ref = rest[-1]
    k = pl.program_id(1)

    @pl.when(k == 0)
    def _():
        x = x_ref[...]
        xbf_ref[...] = x.astype(bf16)
        o_ref[...] = (2.0 * alpha) * x

    gu = jnp.dot(xbf_ref[...], wgu_ref[...], preferred_element_type=f32)
    gate, up = gu[:, :tf], gu[:, tf:]
    h = (gate * jax.nn.sigmoid(gate) * up).astype(bf16)
    o_ref[...] += jnp.dot(h, wo_ref[...], preferred_element_type=f32)

    @pl.when(k == pl.num_programs(1) - 1)
    def _():
        _layer_norm_rows(o_ref, None, g_ref, b_ref, o_ref, obf_ref, scale=1.0, eps=4.0 * LN_EPS)


def _regroup_kernel(g_ref, u_ref, o_ref):
    tf = g_ref.shape[1]
    o_ref[:, :tf] = g_ref[...].astype(bf16)
    o_ref[:, tf:] = u_ref[...].astype(bf16)


def _ffn_weights(w_in, w_out):
    depth, d, two_dff = w_in.shape
    dff = two_dff // 2
    tf = _pick(dff, (FFN_TF, 128))
    nk = dff // tf
    wgu = pl.pallas_call(
        _regroup_kernel,
        grid=(depth, nk),
        in_specs=[
            pl.BlockSpec((None, d, tf), lambda l, k: (l, 0, k)),
            pl.BlockSpec((None, d, tf), lambda l, k: (l, 0, k + nk)),
        ],
        out_specs=pl.BlockSpec((None, None, d, 2 * tf), lambda l, k: (l, k, 0, 0)),
        out_shape=jax.ShapeDtypeStruct((depth, nk, d, 2 * tf), bf16),
        compiler_params=_params(("parallel", "parallel")),
        name="ffn_weight_regroup",
    )(w_in, w_in)
    return wgu, w_out.astype(bf16)


def _ffn(x, row0, nrows, wgu, wo, layer, g, b, alpha, want_bf16, out_rows=None, out_row0=0, prev=()):
    d = x.shape[1]
    out_rows = nrows if out_rows is None else out_rows
    nk, tf = wgu.shape[1], wgu.shape[3] // 2
    tm = _pick(math.gcd(math.gcd(nrows, row0), out_row0), (512, 256, 128))
    roff, ooff = row0 // tm, out_row0 // tm
    out_specs = [pl.BlockSpec((tm, d), lambda i, k: (i + ooff, 0))]
    out_shape = [jax.ShapeDtypeStruct((out_rows, d), f32)]
    if want_bf16:
        out_specs.append(pl.BlockSpec((tm, d), lambda i, k: (i + ooff, 0)))
        out_shape.append(jax.ShapeDtypeStruct((out_rows, d), bf16))
    assert len(prev) in (0, len(out_shape))
    return pl.pallas_call(
        functools.partial(_ffn_kernel, alpha=alpha, tf=tf, n_prev=len(prev), want_bf16=want_bf16),
        grid=(nrows // tm, nk),
        in_specs=[
            pl.BlockSpec((tm, d), lambda i, k: (i + roff, 0)),
            pl.BlockSpec((None, None, d, 2 * tf), lambda i, k: (layer, k, 0, 0)),
            pl.BlockSpec((None, tf, d), lambda i, k: (layer, k, 0)),
            pl.BlockSpec((1, d), lambda i, k: (0, 0)),
            pl.BlockSpec((1, d), lambda i, k: (0, 0)),
        ] + [pl.BlockSpec(memory_space=pl.ANY)] * len(prev),
        out_specs=out_specs,
        out_shape=out_shape,
        input_output_aliases={5 + j: j for j in range(len(prev))},
        scratch_shapes=[pltpu.VMEM((tm, d), bf16)],
        compiler_params=_params(("parallel", "arbitrary")),
        name="ffn",
    )(x, wgu, wo, g, b, *prev)


def _matmul_kernel(x_ref, w_ref, o_ref):
    o_ref[...] = jnp.dot(x_ref[...], w_ref[...], preferred_element_type=f32).astype(o_ref.dtype)


def _matmul(x, w, layer, col0, ncols, out_dtype):
    n, d = x.shape
    tm = _pick(n, (1024, 512, 256, 128))
    tn = _pick(math.gcd(ncols, col0) if col0 else ncols, (1536, 1024, 768, 512, 256, 128))
    off = col0 // tn
    return pl.pallas_call(
        _matmul_kernel,
        grid=(n // tm, ncols // tn),
        in_specs=[
            pl.BlockSpec((tm, d), lambda i, j: (i, 0)),
            pl.BlockSpec((None, d, tn), lambda i, j: (layer, 0, j + off)),
        ],
        out_specs=pl.BlockSpec((tm, tn), lambda i, j: (i, j)),
        out_shape=jax.ShapeDtypeStruct((n, ncols), out_dtype),
        compiler_params=_params(("parallel", "arbitrary")),
        name="in_proj",
    )(x, w)


def _s5_kernel(u_ref, bmat_ref, cmat_ref, a_ref, aseg_ref, d_ref, o_ref,
               stage_ref, perm_ref, s_ref, y_ref, *, seg):
    seqlen, cw = u_ref.shape
    padded = seg * SUBLANES
    half = s_ref.shape[1] // 2
    slabs = [slice(sl * LANES, (sl + 1) * LANES) for sl in range(cw // LANES)]

    for sl, cols in enumerate(slabs):
        stage_ref[sl, 0:seqlen, :] = u_ref[:, cols]
        stage_ref[sl, seqlen:padded, :] = jnp.zeros((padded - seqlen, LANES), f32)

    def perm(i, c):
        row = pl.multiple_of(i * SUBLANES, SUBLANES)
        for sl, cols in enumerate(slabs):
            perm_ref[pl.ds(row, SUBLANES), cols] = stage_ref[sl, pl.ds(i, SUBLANES, stride=seg), :]
        return c
    lax.fori_loop(0, seg, perm, 0)

    up = perm_ref[...]
    ub = up.astype(bf16)
    y_ref[...] = d_ref[...] * up
    rowid = lax.broadcasted_iota(jnp.int32, (SUBLANES, half), 0)
    zero = jnp.zeros((SUBLANES, half), f32)

    for dirn in (0, 1):
        rev = dirn == 1
        s_ref[...] = jnp.dot(ub, bmat_ref[dirn], preferred_element_type=f32)
        a_re, a_im = a_ref[dirn, :, :half], a_ref[dirn, :, half:]

        def scan(n, carry, store, rev=rev, a_re=a_re, a_im=a_im):
            sr, si = carry
            i = seg - 1 - n if rev else n
            row = pl.multiple_of(i * SUBLANES, SUBLANES)
            nr = a_re * sr - a_im * si + s_ref[pl.ds(row, SUBLANES), :half]
            ni = a_re * si + a_im * sr + s_ref[pl.ds(row, SUBLANES), half:]
            if store:
                s_ref[pl.ds(row, SUBLANES), :half] = nr
                s_ref[pl.ds(row, SUBLANES), half:] = ni
            return nr, ni

        end_r, end_i = lax.fori_loop(0, seg, functools.partial(scan, store=False),
                                     (zero, zero), unroll=2)

        p_re, p_im = aseg_ref[dirn, :, :half], aseg_ref[dirn, :, half:]
        x_re, x_im = zero, zero
        edge = SUBLANES - 1 if rev else 0
        shift = SUBLANES - 1 if rev else 1
        for _ in range(SUBLANES - 1):
            t_re = end_r + p_re * x_re - p_im * x_im
            t_im = end_i + p_re * x_im + p_im * x_re
            x_re = jnp.where(rowid == edge, 0.0, pltpu.roll(t_re, shift, 0))
            x_im = jnp.where(rowid == edge, 0.0, pltpu.roll(t_im, shift, 0))

        lax.fori_loop(0, seg, functools.partial(scan, store=True), (x_re, x_im), unroll=2)

        y_ref[...] += jnp.dot(s_ref[...].astype(bf16), cmat_ref[dirn], preferred_element_type=f32)

    y_ref[...] = jax.nn.gelu(y_ref[...])

    def unperm(i, c):
        row = pl.multiple_of(i * SUBLANES, SUBLANES)
        for sl, cols in enumerate(slabs):
            stage_ref[sl, pl.ds(i, SUBLANES, stride=seg), :] = y_ref[pl.ds(row, SUBLANES), cols]
        return c
    lax.fori_loop(0, seg, unperm, 0)
    for sl, cols in enumerate(slabs):
        o_ref[:, cols] = stage_ref[sl, 0:seqlen, :]


def _s5_tables(lam_re, lam_im, log_step, b_re, b_im, c_re, c_im, seg):
    depth, _, g, p = lam_re.shape
    h = b_re.shape[-1]
    gb = S5_GROUPS_PER_BLOCK
    nb = g // gb
    lr, li = lam_re.astype(f32), lam_im.astype(f32)
    step = jnp.exp(log_step.astype(f32))[..., None]
    mag = jnp.exp(lr * step)
    ang = li * step
    ar, ai = mag * jnp.cos(ang), mag * jnp.sin(ang)
    den = lr * lr + li * li
    fr = ((ar - 1.0) * lr + ai * li) / den
    fi = (ai * lr - (ar - 1.0) * li) / den
    br, bi = b_re.astype(f32), b_im.astype(f32)
    bbar = jnp.stack([fr[..., None] * br - fi[..., None] * bi,
                      fr[..., None] * bi + fi[..., None] * br], axis=-1)
    eye = jnp.eye(gb, dtype=f32)
    bmat = jnp.einsum('ldjgphc,gk->ldjghckp', bbar.reshape(depth, 2, nb, gb, p, h, 2), eye)
    bmat = bmat.reshape(depth, 2, nb, gb * h, 2 * gb * p).astype(bf16)
    cc = jnp.stack([c_re.astype(f32), -c_im.astype(f32)], axis=-1)
    cmat = jnp.einsum('ldjghpc,gk->ldjcgpkh', cc.reshape(depth, 2, nb, gb, h, p, 2), eye)
    cmat = cmat.reshape(depth, 2, nb, 2 * gb * p, gb * h).astype(bf16)

    def rows(re, im):
        v = jnp.concatenate([re.reshape(depth, 2, nb, 1, gb * p), im.reshape(depth, 2, nb, 1, gb * p)],
                            axis=-1)
        return jnp.broadcast_to(v, (depth, 2, nb, SUBLANES, 2 * gb * p))

    a = rows(ar, ai)
    smag = jnp.exp(seg * (lr * step))
    aseg = rows(smag * jnp.cos(seg * ang), smag * jnp.sin(seg * ang))
    return bmat, cmat, a, aseg


def _s5(proj, nseq, seqlen, tables, layer, d, width):
    bmat, cmat, a, aseg = tables
    n = proj.shape[0]
    nb, cw, sw = bmat.shape[2], bmat.shape[3], bmat.shape[4]
    seg = seqlen // SUBLANES + S5_SEG_PAD
    padded = seg * SUBLANES
    return pl.pallas_call(
        functools.partial(_s5_kernel, seg=seg),
        grid=(nseq, nb),
        in_specs=[
            pl.BlockSpec((seqlen, cw), lambda b, j: (b, j)),
            pl.BlockSpec((None, 2, None, cw, sw), lambda b, j: (layer, 0, j, 0, 0)),
            pl.BlockSpec((None, 2, None, sw, cw), lambda b, j: (layer, 0, j, 0, 0)),
            pl.BlockSpec((None, 2, None, SUBLANES, sw), lambda b, j: (layer, 0, j, 0, 0)),
            pl.BlockSpec((None, 2, None, SUBLANES, sw), lambda b, j: (layer, 0, j, 0, 0)),
            pl.BlockSpec((1, cw), lambda b, j: (0, j)),
        ],
        out_specs=pl.BlockSpec((seqlen, cw), lambda b, j: (b, j)),
        out_shape=jax.ShapeDtypeStruct((n, width), f32),
        scratch_shapes=[pltpu.VMEM((cw // LANES, padded, LANES), f32), pltpu.VMEM((padded, cw), f32),
                        pltpu.VMEM((padded, sw), f32), pltpu.VMEM((padded, cw), f32)],
        compiler_params=_params(("parallel", "arbitrary")),
        name="s5_scan",
    )(proj, bmat, cmat, a, aseg, d)


def _glu_kernel(z_ref, w_ref, b_ref, o_ref):
    z = z_ref[...]
    gate = jnp.dot(z.astype(bf16), w_ref[...], preferred_element_type=f32) + b_ref[...]
    o_ref[...] = (z * jax.nn.sigmoid(gate)).astype(o_ref.dtype)


def _glu(z, w, layer, b):
    n, c = z.shape
    tm = _pick(n, (1024, 512, 256, 128))
    return pl.pallas_call(
        _glu_kernel,
        grid=(n // tm,),
        in_specs=[
            pl.BlockSpec((tm, c), lambda i: (i, 0)),
            pl.BlockSpec((None, c, c), lambda i: (layer, 0, 0)),
            pl.BlockSpec((1, c), lambda i: (0, 0)),
        ],
        out_specs=pl.BlockSpec((tm, c), lambda i: (i, 0)),
        out_shape=jax.ShapeDtypeStruct((n, c), bf16),
        compiler_params=_params(("parallel",)),
        name="s5_glu",
    )(z, w, b)


CONV_ROWS = 64
CONV_HALO = 16


def _conv_kernel(v_ref, g_ref, w_ref, b_ref, o_ref, hp_ref, *, taps, pad):
    seqlen, cw = v_ref.shape
    hp_ref[0:CONV_HALO, :] = jnp.zeros((CONV_HALO, cw), f32)
    hp_ref[seqlen + CONV_HALO:seqlen + 2 * CONV_HALO, :] = jnp.zeros((CONV_HALO, cw), f32)
    hp_ref[CONV_HALO:seqlen + CONV_HALO, :] = v_ref[...] * jax.nn.sigmoid(g_ref[...])
    groups = CONV_ROWS // SUBLANES
    bias = jnp.broadcast_to(b_ref[...], (groups, SUBLANES, cw))
    first = CONV_HALO - pad
    span = (first + taps - 1) // SUBLANES * SUBLANES + CONV_ROWS

    def tile(i, c):
        base = pl.multiple_of(i * CONV_ROWS, CONV_ROWS)
        win = hp_ref[pl.ds(base, CONV_ROWS + 2 * CONV_HALO), :]
        acc = bias
        for phase in range(SUBLANES):
            ks = [k for k in range(taps) if (first + k) % SUBLANES == phase]
            if not ks:
                continue
            shifted = win[phase:phase + span, :]
            for k in ks:
                q = (first + k) // SUBLANES * SUBLANES
                acc = acc + w_ref[k] * shifted[q:q + CONV_ROWS, :].reshape(groups, SUBLANES, cw)
        o_ref[pl.ds(base, CONV_ROWS), :] = acc.reshape(CONV_ROWS, cw)
        return c
    lax.fori_loop(0, seqlen // CONV_ROWS, tile, 0)


def _conv(proj, nseq, seqlen, v_col0, g_col0, width, conv_w, layer, conv_b):
    n = proj.shape[0]
    taps = conv_w.shape[1]
    pad = (taps - 1) // 2
    assert pad <= CONV_HALO and CONV_HALO - pad + taps - 1 + CONV_ROWS <= CONV_ROWS + 2 * CONV_HALO
    cw = _pick(width, (256, 128))
    voff, goff = v_col0 // cw, g_col0 // cw
    return pl.pallas_call(
        functools.partial(_conv_kernel, taps=taps, pad=pad),
        grid=(nseq, width // cw),
        in_specs=[
            pl.BlockSpec((seqlen, cw), lambda b, j: (b, j + voff)),
            pl.BlockSpec((seqlen, cw), lambda b, j: (b, j + goff)),
            pl.BlockSpec((None, taps, SUBLANES, cw), lambda b, j: (layer, 0, 0, j)),
            pl.BlockSpec((1, cw), lambda b, j: (0, j)),
        ],
        out_specs=pl.BlockSpec((seqlen, cw), lambda b, j: (b, j)),
        out_shape=jax.ShapeDtypeStruct((n, width), f32),
        scratch_shapes=[pltpu.VMEM((seqlen + 2 * CONV_HALO, cw), f32)],
        compiler_params=_params(("parallel", "arbitrary")),
        name="conv",
    )(proj, proj, conv_w, conv_b)


def _ln_silu_kernel(x_ref, g_ref, b_ref, o_ref):
    y = _layer_norm(x_ref[...], g_ref[...], b_ref[...])
    o_ref[...] = (y * jax.nn.sigmoid(y)).astype(o_ref.dtype)


def _ln_silu(x, g, b):
    n, c = x.shape
    tm = _pick(n, (1024, 512, 256, 128))
    return pl.pallas_call(
        _ln_silu_kernel,
        grid=(n // tm,),
        in_specs=[
            pl.BlockSpec((tm, c), lambda i: (i, 0)),
            pl.BlockSpec((1, c), lambda i: (0, 0)),
            pl.BlockSpec((1, c), lambda i: (0, 0)),
        ],
        out_specs=pl.BlockSpec((tm, c), lambda i: (i, 0)),
        out_shape=jax.ShapeDtypeStruct((n, c), bf16),
        compiler_params=_params(("parallel",)),
        name="conv_ln",
    )(x, g, b)


def _attn_kernel(slopes_ref, lam_ref, q_ref, k_ref, v_ref, g_ref, o_ref, bias_ref, *, scale, post, dh):
    h = pl.program_id(0)
    b = pl.program_id(1)
    i = pl.program_id(2)
    tq = q_ref.shape[0]
    seqlen = k_ref.shape[0]
    lam = lam_ref[0]

    @pl.when(b == 0)
    def _():
        rel = (lax.broadcasted_iota(jnp.int32, (tq, seqlen), 0)
               - lax.broadcasted_iota(jnp.int32, (tq, seqlen), 1))
        bias_ref[i] = (slopes_ref[h] * LOG2E) * jnp.abs(rel + i * tq).astype(f32)

    for r in range(tq // ATTN_ROWS):
        rows = slice(r * ATTN_ROWS, (r + 1) * ATTN_ROWS)
        bias = bias_ref[i, rows, :]
        es, rs = [], []
        for c in (0, 1):
            s = lax.dot_general(q_ref[rows, c * dh:(c + 1) * dh], k_ref[:, c * dh:(c + 1) * dh],
                                (((1,), (1,)), ((), ())), preferred_element_type=f32)
            s = s * (scale * LOG2E) - bias
            e = jnp.exp2(s - jnp.max(s, axis=-1, keepdims=True))
            es.append(e.astype(bf16))
            rs.append(1.0 / jnp.sum(e, axis=-1, keepdims=True))
        ov = jnp.dot(jnp.concatenate(es, axis=0), v_ref[...], preferred_element_type=f32)
        o = ov[:ATTN_ROWS] * rs[0] - ov[ATTN_ROWS:] * (lam * rs[1])
        ms = jnp.mean(o * o, axis=-1, keepdims=True)
        o_ref[rows, :] = (o * lax.rsqrt(ms + LN_EPS) * g_ref[...] * post).astype(o_ref.dtype)


def _attn(qkv, nseq, seqlen, heads, dh, slopes, lam, g, post):
    n = qkv.shape[0]
    hw = 2 * dh
    tq = _pick(seqlen, (2048, 1024, 512, 256, 128))
    nq = seqlen // tq
    return pl.pallas_call(
        functools.partial(_attn_kernel, scale=dh ** -0.5, post=post, dh=dh),
        grid=(heads, nseq, nq),
        in_specs=[
            pl.BlockSpec(memory_space=pltpu.SMEM),
            pl.BlockSpec(memory_space=pltpu.SMEM),
            pl.BlockSpec((tq, hw), lambda h, b, i: (b * nq + i, h)),
            pl.BlockSpec((seqlen, hw), lambda h, b, i: (b, heads + h)),
            pl.BlockSpec((seqlen, hw), lambda h, b, i: (b, 2 * heads + h)),
            pl.BlockSpec((1, hw), lambda h, b, i: (0, 0)),
        ],
        out_specs=pl.BlockSpec((tq, hw), lambda h, b, i: (b * nq + i, h)),
        out_shape=jax.ShapeDtypeStruct((n, heads * hw), bf16),
        scratch_shapes=[pltpu.VMEM((nq, tq, seqlen), f32)],
        compiler_params=_params(("arbitrary", "arbitrary", "arbitrary")),
        name="diff_attn",
    )(slopes, lam, qkv, qkv, qkv, g)


def _out_proj_kernel(ya_ref, yb_ref, yc_ref, w_ref, x_ref, g_ref, b_ref, o_ref, *, alpha):
    ca, cb = ya_ref.shape[1], yb_ref.shape[1]
    mix = jnp.dot(ya_ref[...], w_ref[0:ca, :], preferred_element_type=f32)
    mix += jnp.dot(yb_ref[...], w_ref[ca:ca + cb, :], preferred_element_type=f32)
    mix += jnp.dot(yc_ref[...], w_ref[ca + cb:, :], preferred_element_type=f32)
    o_ref[...] = mix
    _layer_norm_rows(o_ref, x_ref, g_ref, b_ref, o_ref, None, scale=alpha, eps=LN_EPS)


def _out_proj(ya, yb, yc, w, layer, x, g, b, alpha):
    n, d = x.shape
    tm = _pick(n, (256, 128))
    row_spec = lambda a: pl.BlockSpec((tm, a.shape[1]), lambda i: (i, 0))
    return pl.pallas_call(
        functools.partial(_out_proj_kernel, alpha=alpha),
        grid=(n // tm,),
        in_specs=[
            row_spec(ya), row_spec(yb), row_spec(yc),
            pl.BlockSpec((None,) + w.shape[1:], lambda i: (layer, 0, 0), pipeline_mode=pl.Buffered(1)),
            row_spec(x),
            pl.BlockSpec((1, d), lambda i: (0, 0)),
            pl.BlockSpec((1, d), lambda i: (0, 0)),
        ],
        out_specs=pl.BlockSpec((tm, d), lambda i: (i, 0)),
        out_shape=jax.ShapeDtypeStruct((n, d), f32),
        compiler_params=_params(("parallel",)),
        name="out_proj",
    )(ya, yb, yc, w, x, g, b)


def kernel(x_prompt, x_sample, ffn1_w_in, ffn1_w_out, ln_ffn1_g, ln_ffn1_b, w_in, s5_lambda_re, s5_lambda_im, s5_log_step, s5_b_re, s5_b_im, s5_c_re, s5_c_im, s5_d, s5_glu_w, s5_glu_b, conv_w, conv_b, conv_ln_g, conv_ln_b, attn_lambda_q1, attn_lambda_k1, attn_lambda_q2, attn_lambda_k2, attn_subln_g, w_out, ln_mix_g, ln_mix_b, ffn2_w_in, ffn2_w_out, ln_ffn2_g, ln_ffn2_b):
    depth = w_in.shape[0]
    bp, seqlen, d_model = x_prompt.shape
    bs, seqlen_s, _ = x_sample.shape
    assert seqlen == seqlen_s and seqlen % (SUBLANES * SUBLANES) == 0
    nseq = bp + bs
    n_p, n_s = bp * seqlen, bs * seqlen
    n = n_p + n_s
    alpha = (2 * depth) ** 0.25

    s5_w = s5_d.shape[1]
    conv_wd = conv_b.shape[1]
    dh = attn_lambda_q1.shape[1]
    attn_w = d_model - s5_w - conv_wd
    heads = attn_w // (2 * dh)
    f32_cols = s5_w + 2 * conv_wd
    seg = seqlen // SUBLANES + S5_SEG_PAD

    slopes = jnp.asarray([2.0 ** (-8.0 * (h + 1) / heads) for h in range(heads)], dtype=f32)
    row = lambda v: v.astype(f32).reshape(1, -1)

    ffn1_wgu, ffn1_wo = _ffn_weights(ffn1_w_in, ffn1_w_out)
    ffn2_wgu, ffn2_wo = _ffn_weights(ffn2_w_in, ffn2_w_out)
    w_in_b, w_out_b, glu_w_b = w_in.astype(bf16), w_out.astype(bf16), s5_glu_w.astype(bf16)
    conv_w_f = jnp.broadcast_to(conv_w.astype(f32)[:, :, None, :],
                                conv_w.shape[:2] + (SUBLANES, conv_wd))
    tables = _s5_tables(s5_lambda_re, s5_lambda_im, s5_log_step, s5_b_re, s5_b_im, s5_c_re,
                        s5_c_im, seg)

    for l in range(depth):
        g1, b1 = row(ln_ffn1_g[l]), row(ln_ffn1_b[l])
        if l == 0:
            first = _ffn(x_prompt.reshape(n_p, d_model).astype(f32), 0, n_p, ffn1_wgu, ffn1_wo, l,
                         g1, b1, alpha, True, out_rows=n)
            x, xb = _ffn(x_sample.reshape(n_s, d_model).astype(f32), 0, n_s, ffn1_wgu, ffn1_wo, l,
                         g1, b1, alpha, True, out_rows=n, out_row0=n_p, prev=tuple(first))
        else:
            x, xb = _ffn(x, 0, n, ffn1_wgu, ffn1_wo, l, g1, b1, alpha, True)

        proj = _matmul(xb, w_in_b, l, 0, f32_cols, f32)
        qkv = _matmul(xb, w_in_b, l, f32_cols, 3 * attn_w, bf16)

        z = _s5(proj, nseq, seqlen, tables, l, row(s5_d[l]), s5_w)
        y_a = _glu(z, glu_w_b, l, row(s5_glu_b[l]))

        c = _conv(proj, nseq, seqlen, s5_w, s5_w + conv_wd, conv_wd, conv_w_f, l, row(conv_b[l]))
        y_b = _ln_silu(c, row(conv_ln_g[l]), row(conv_ln_b[l]))

        lam_init = 0.8 - 0.6 * math.exp(-0.3 * l)
        lam = (jnp.exp(jnp.sum(attn_lambda_q1[l].astype(f32) * attn_lambda_k1[l].astype(f32)))
               - jnp.exp(jnp.sum(attn_lambda_q2[l].astype(f32) * attn_lambda_k2[l].astype(f32)))
               + lam_init).reshape(1)
        y_c = _attn(qkv, nseq, seqlen, heads, dh, slopes, lam, row(attn_subln_g[l]), 1.0 - lam_init)

        x = _out_proj(y_a, y_b, y_c, w_out_b, l, x, row(ln_mix_g[l]), row(ln_mix_b[l]), alpha)

        g2, b2 = row(ln_ffn2_g[l]), row(ln_ffn2_b[l])
        if l + 1 < depth:
            x, = _ffn(x, 0, n, ffn2_wgu, ffn2_wo, l, g2, b2, alpha, False)
        else:
            y_p, = _ffn(x, 0, n_p, ffn2_wgu, ffn2_wo, l, g2, b2, alpha, False)
            y_s, = _ffn(x, n_p, n_s, ffn2_wgu, ffn2_wo, l, g2, b2, alpha, False)

    y_prompt = y_p.reshape(bp, seqlen, d_model).astype(x_prompt.dtype)
    y_sample = y_s.reshape(bs, seqlen, d_model).astype(x_sample.dtype)
    return (y_prompt, y_sample)
```
